```python
import math
import jax, jax.numpy as jnp
from jax import lax
import numpy as np

D_MODEL = 1024
BATCH = 4
SEQ = 8192
DEPTH = 1
DEC_BATCH = 32
DEC_SEQ = 2048
PAST_LEN = 128

HEAD_DIM = 64
N_Q_HEADS = 16
N_KV_HEADS = 4
GQA_GROUP = N_Q_HEADS // N_KV_HEADS
ATTN_WIDTH = N_Q_HEADS * HEAD_DIM
KV_WIDTH = N_KV_HEADS * HEAD_DIM
WINDOW = 128
BLOCK = 128
ROPE_THETA = 10000.0
SSM_HEADS = 8
SSM_HEAD_DIM = 64
SSM_INNER = SSM_HEADS * SSM_HEAD_DIM
SSM_GROUPS = 2
SSM_STATE = 128
SSM_CONV = 5
SSM_CHUNK = 128
XBC_WIDTH = SSM_INNER + 2 * SSM_GROUPS * SSM_STATE
DT_WIDTH = 2 * SSM_HEADS
N_MEM = 256
MEM_HEADS = 4
MEM_HEAD_DIM = 128
MEM_WIDTH = MEM_HEADS * MEM_HEAD_DIM
MIX_WIDTH = ATTN_WIDTH + SSM_INNER + MEM_WIDTH
IN_WIDTH = ATTN_WIDTH + 2 * KV_WIDTH + SSM_INNER + XBC_WIDTH + DT_WIDTH + MEM_WIDTH
D_FF = 2816
FFN_CONV = 3
EPS = 1e-6

kernel_name = "hymba_window_ssd_memory_encoder"


def rms_norm(x, w):
    xf = x.astype(jnp.float32)
    y = xf * lax.rsqrt(jnp.mean(xf * xf, axis=-1, keepdims=True) + EPS)
    return (y * w.astype(jnp.float32)).astype(x.dtype)


def rope(x, pos):
    d = x.shape[-1]
    inv = ROPE_THETA ** (-jnp.arange(0, d, 2, dtype=jnp.float32) / d)
    ang = pos.astype(jnp.float32)[:, None] * inv[None, :]
    cos = jnp.cos(ang)[None, :, None, :]
    sin = jnp.sin(ang)[None, :, None, :]
    xf = x.astype(jnp.float32)
    x1, x2 = xf[..., : d // 2], xf[..., d // 2:]
    return jnp.concatenate([x1 * cos - x2 * sin, x2 * cos + x1 * sin], axis=-1).astype(x.dtype)


def dwconv_centred(x, w, b):
    k = w.shape[0]
    pad = k // 2
    s = x.shape[1]
    xp = jnp.pad(x, ((0, 0), (pad, pad), (0, 0)))
    y = xp[:, 0:s] * w[0].astype(x.dtype) + b.astype(x.dtype)
    for t in range(1, k):
        y = y + xp[:, t:t + s] * w[t].astype(x.dtype)
    return y


def windowed_gqa(q, k, v, sink):
    b, s, _, dh = q.shape
    nb = s // BLOCK
    qb = jnp.moveaxis(q.reshape(b, nb, BLOCK, N_KV_HEADS, GQA_GROUP, dh), 1, 0)
    kp = jnp.pad(k, ((0, 0), (BLOCK, BLOCK), (0, 0), (0, 0)))
    vp = jnp.pad(v, ((0, 0), (BLOCK, BLOCK), (0, 0), (0, 0)))
    sink_f = sink.astype(jnp.float32).reshape(N_KV_HEADS, GQA_GROUP)[None, :, :, None]
    scale = dh ** -0.5
    q_off = jnp.arange(BLOCK)
    k_off = jnp.arange(3 * BLOCK)
    band = jnp.abs((k_off[None, :] - BLOCK) - q_off[:, None]) <= WINDOW

    def one_block(args):
        i, qi = args
        start = i * BLOCK
        ki = lax.dynamic_slice_in_dim(kp, start, 3 * BLOCK, axis=1)
        vi = lax.dynamic_slice_in_dim(vp, start, 3 * BLOCK, axis=1)
        kpos = start - BLOCK + k_off
        valid = band & ((kpos >= 0) & (kpos < s))[None, :]
        sc = jnp.einsum('bqhgd,bkhd->bhgqk', qi, ki, preferred_element_type=jnp.float32) * scale
        sc = jnp.where(valid, sc, -jnp.inf)
        m = jnp.maximum(jnp.max(sc, axis=-1), sink_f)
        p = jnp.exp(sc - m[..., None])
        denom = jnp.sum(p, axis=-1) + jnp.exp(sink_f - m)
        p = p / denom[..., None]
        o = jnp.einsum('bhgqk,bkhd->bqhgd', p, vi.astype(jnp.float32))
        return o.astype(q.dtype)

    out = lax.map(one_block, (jnp.arange(nb), qb))
    return jnp.moveaxis(out, 0, 1).reshape(b, s, N_Q_HEADS * dh)


def segsum_exp(a):
    t = a.shape[-1]
    cs = jnp.cumsum(a, axis=-1)
    diff = cs[..., :, None] - cs[..., None, :]
    mask = jnp.tril(jnp.ones((t, t), dtype=bool))
    return jnp.exp(jnp.where(mask, diff, -jnp.inf))


def ssd_scan(x, dt, a_neg, bm, cm):
    b, s, nh, hp = x.shape
    nc = s // SSM_CHUNK
    ln = SSM_CHUNK
    e = nh // SSM_GROUPS
    xf = (x.astype(jnp.float32) * dt[..., None]).reshape(b, nc, ln, SSM_GROUPS, e, hp)
    a = jnp.transpose((dt * a_neg).reshape(b, nc, ln, SSM_GROUPS, e), (0, 1, 3, 4, 2))
    bf = bm.astype(jnp.float32).reshape(b, nc, ln, SSM_GROUPS, SSM_STATE)
    cf = cm.astype(jnp.float32).reshape(b, nc, ln, SSM_GROUPS, SSM_STATE)
    a_cs = jnp.cumsum(a, axis=-1)
    lmat = segsum_exp(a)
    cb = jnp.einsum('bclgn,bcsgn->bcgls', cf, bf)
    y_diag = jnp.einsum('bcgls,bcgels,bcsgep->bclgep', cb, lmat, xf)
    decay_states = jnp.exp(a_cs[..., -1:] - a_cs)
    states = jnp.einsum('bclgn,bcgel,bclgep->bcgepn', bf, decay_states, xf)
    chunk_decay = jnp.exp(a_cs[..., -1])

    def step(h, inp):
        st, dec = inp
        return h * dec[..., None, None] + st, h

    h0 = jnp.zeros((b, SSM_GROUPS, e, hp, SSM_STATE), jnp.float32)
    _, h_prev = lax.scan(step, h0, (jnp.moveaxis(states, 1, 0), jnp.moveaxis(chunk_decay, 1, 0)))
    h_prev = jnp.moveaxis(h_prev, 0, 1)
    y_off = jnp.einsum('bclgn,bcgepn,bcgel->bclgep', cf, h_prev, jnp.exp(a_cs))
    return (y_diag + y_off).reshape(b, s, nh, hp)


def ssd_bidirectional(z, xbc, dt_raw, conv_w, conv_b, dt_bias, a_log, d_skip, norm_w):
    b, s, _ = z.shape
    xbc = jax.nn.silu(dwconv_centred(xbc, conv_w, conv_b))
    xs = xbc[..., :SSM_INNER].reshape(b, s, SSM_HEADS, SSM_HEAD_DIM)
    bm = xbc[..., SSM_INNER:SSM_INNER + SSM_GROUPS * SSM_STATE].reshape(b, s, SSM_GROUPS, SSM_STATE)
    cm = xbc[..., SSM_INNER + SSM_GROUPS * SSM_STATE:].reshape(b, s, SSM_GROUPS, SSM_STATE)
    dt = jax.nn.softplus(dt_raw.astype(jnp.float32).reshape(b, s, 2, SSM_HEADS)
                         + dt_bias.astype(jnp.float32))
    a_neg = -jnp.exp(a_log.astype(jnp.float32))
    y_f = ssd_scan(xs, dt[:, :, 0], a_neg[0], bm, cm)
    y_b = ssd_scan(xs[:, ::-1], dt[:, ::-1, 1], a_neg[1], bm[:, ::-1], cm[:, ::-1])[:, ::-1]
    y = y_f + y_b + d_skip.astype(jnp.float32)[:, None] * xs.astype(jnp.float32)
    y = y.reshape(b, s, SSM_INNER) * jax.nn.silu(z.astype(jnp.float32))
    return rms_norm(y, norm_w).astype(z.dtype)


def memory_attention(mq, mk, mv):
    b, s = mq.shape[:2]
    sc = jnp.einsum('bshd,bmhd->bhsm', mq, mk, preferred_element_type=jnp.float32) * (MEM_HEAD_DIM ** -0.5)
    p = jax.nn.softmax(sc, axis=-1)
    o = jnp.einsum('bhsm,bmhd->bshd', p, mv.astype(jnp.float32))
    return o.astype(mq.dtype).reshape(b, s, MEM_WIDTH)


def encoder_layer(x, mem, norm1_w, w_in, q_norm_w, k_norm_w, attn_sink, ssm_conv_w, ssm_conv_b,
                  ssm_dt_bias, ssm_A_log, ssm_D, ssm_norm_w, mem_norm_w, w_mem_kv, mq_norm_w,
                  mk_norm_w, w_out, norm2_w, w_ffn_up, ffn_conv_w, ffn_conv_b, w_ffn_down):
    b, s, _ = x.shape
    h = rms_norm(x, norm1_w)
    proj = h @ w_in
    sizes = (ATTN_WIDTH, KV_WIDTH, KV_WIDTH, SSM_INNER, XBC_WIDTH, DT_WIDTH)
    q, k, v, z, xbc, dt_raw, mq = jnp.split(proj, np.cumsum(sizes).tolist(), axis=-1)
    pos = jnp.arange(s)
    q = rope(rms_norm(q.reshape(b, s, N_Q_HEADS, HEAD_DIM), q_norm_w), pos)
    k = rope(rms_norm(k.reshape(b, s, N_KV_HEADS, HEAD_DIM), k_norm_w), pos)
    v = v.reshape(b, s, N_KV_HEADS, HEAD_DIM)
    attn_out = windowed_gqa(q, k, v, attn_sink)
    ssm_out = ssd_bidirectional(z, xbc, dt_raw, ssm_conv_w, ssm_conv_b, ssm_dt_bias, ssm_A_log, ssm_D, ssm_norm_w)
    m = mem.shape[1]
    mkv = rms_norm(mem, mem_norm_w) @ w_mem_kv
    mk = rms_norm(mkv[..., :MEM_WIDTH].reshape(b, m, MEM_HEADS, MEM_HEAD_DIM), mk_norm_w)
    mv = mkv[..., MEM_WIDTH:].reshape(b, m, MEM_HEADS, MEM_HEAD_DIM)
    mq = rms_norm(mq.reshape(b, s, MEM_HEADS, MEM_HEAD_DIM), mq_norm_w)
    mem_out = memory_attention(mq, mk, mv)
    mix = jnp.concatenate([attn_out, ssm_out, mem_out], axis=-1)
    x = x + mix @ w_out
    u = dwconv_centred(rms_norm(x, norm2_w) @ w_ffn_up, ffn_conv_w, ffn_conv_b)
    gate, up = u[..., :D_FF], u[..., D_FF:]
    return x + (jax.nn.silu(gate) * up) @ w_ffn_down


def setup_inputs(seed: int = 0) -> dict:
    key = jax.random.key(seed)
    ks = jax.random.split(key, 32)
    f32 = jnp.float32

    def nrm(k, shape, scale):
        return jax.random.normal(k, shape, f32) * scale

    def gain(k, shape):
        return 1.0 + 0.02 * jax.random.normal(k, shape, f32)

    dt0 = jnp.exp(jax.random.uniform(ks[12], (DEPTH, 2, SSM_HEADS), f32, math.log(1e-3), math.log(1e-1)))
    return {
        "x_prompt": nrm(ks[0], (BATCH, SEQ, D_MODEL), 1.0),
        "x_sample": nrm(ks[1], (DEC_BATCH, DEC_SEQ, D_MODEL), 1.0),
        "mem_prompt": nrm(ks[2], (BATCH, N_MEM, D_MODEL), 1.0),
        "mem_sample": nrm(ks[3], (DEC_BATCH, N_MEM, D_MODEL), 1.0),
        "norm1_w": gain(ks[4], (DEPTH, D_MODEL)),
        "w_in": nrm(ks[5], (DEPTH, D_MODEL, IN_WIDTH), D_MODEL ** -0.5),
        "q_norm_w": gain(ks[6], (DEPTH, HEAD_DIM)),
        "k_norm_w": gain(ks[7], (DEPTH, HEAD_DIM)),
        "attn_sink": nrm(ks[8], (DEPTH, N_Q_HEADS), 0.5),
        "ssm_conv_w": nrm(ks[9], (DEPTH, SSM_CONV, XBC_WIDTH), SSM_CONV ** -0.5),
        "ssm_conv_b": nrm(ks[10], (DEPTH, XBC_WIDTH), 0.01),
        "ssm_dt_bias": dt0 + jnp.log(-jnp.expm1(-dt0)),
        "ssm_A_log": jnp.log(jax.random.uniform(ks[11], (DEPTH, 2, SSM_HEADS), f32, 1.0, 16.0)),
        "ssm_D": gain(ks[13], (DEPTH, SSM_HEADS)),
        "ssm_norm_w": gain(ks[14], (DEPTH, SSM_INNER)),
        "mem_norm_w": gain(ks[15], (DEPTH, D_MODEL)),
        "w_mem_kv": nrm(ks[16], (DEPTH, D_MODEL, 2 * MEM_WIDTH), D_MODEL ** -0.5),
        "mq_norm_w": gain(ks[17], (DEPTH, MEM_HEAD_DIM)),
        "mk_norm_w": gain(ks[18], (DEPTH, MEM_HEAD_DIM)),
        "w_out": nrm(ks[19], (DEPTH, MIX_WIDTH, D_MODEL), MIX_WIDTH ** -0.5),
        "norm2_w": gain(ks[20], (DEPTH, D_MODEL)),
        "w_ffn_up": nrm(ks[21], (DEPTH, D_MODEL, 2 * D_FF), D_MODEL ** -0.5),
        "ffn_conv_w": nrm(ks[22], (DEPTH, FFN_CONV, 2 * D_FF), FFN_CONV ** -0.5),
        "ffn_conv_b": nrm(ks[23], (DEPTH, 2 * D_FF), 0.01),
        "w_ffn_down": nrm(ks[24], (DEPTH, D_FF, D_MODEL), D_FF ** -0.5),
    }


def reference(x_prompt, x_sample, mem_prompt, mem_sample, norm1_w, w_in, q_norm_w, k_norm_w, attn_sink,
              ssm_conv_w, ssm_conv_b, ssm_dt_bias, ssm_A_log, ssm_D, ssm_norm_w, mem_norm_w, w_mem_kv,
              mq_norm_w, mk_norm_w, w_out, norm2_w, w_ffn_up, ffn_conv_w, ffn_conv_b, w_ffn_down):
    weights = (norm1_w, w_in, q_norm_w, k_norm_w, attn_sink, ssm_conv_w, ssm_conv_b, ssm_dt_bias,
               ssm_A_log, ssm_D, ssm_norm_w, mem_norm_w, w_mem_kv, mq_norm_w, mk_norm_w, w_out,
               norm2_w, w_ffn_up, ffn_conv_w, ffn_conv_b, w_ffn_down)
    y_prompt = x_prompt
    y_sample = x_sample
    for layer in range(DEPTH):
        layer_w = [w[layer] for w in weights]
        y_prompt = encoder_layer(y_prompt, mem_prompt, *layer_w)
        y_sample = encoder_layer(y_sample, mem_sample, *layer_w)
    return (y_prompt, y_sample)
```

```python
import functools
import math

import numpy as np
import jax
import jax.numpy as jnp
from jax import lax
from jax.experimental import pallas as pl
from jax.experimental.pallas import tpu as pltpu

F32 = jnp.float32
BF16 = jnp.bfloat16

D_MODEL = 1024
HEAD_DIM = 64
HALF = HEAD_DIM // 2
N_Q_HEADS = 16
N_KV_HEADS = 4
GQA_GROUP = N_Q_HEADS // N_KV_HEADS
ATTN_WIDTH = N_Q_HEADS * HEAD_DIM
KV_WIDTH = N_KV_HEADS * HEAD_DIM
WINDOW = 128
BLOCK = 128
ROPE_THETA = 10000.0
SSM_HEADS = 8
SSM_HEAD_DIM = 64
SSM_INNER = SSM_HEADS * SSM_HEAD_DIM
SSM_GROUPS = 2
SSM_STATE = 128
SSM_CONV = 5
SSM_CHUNK = 128
XBC_WIDTH = SSM_INNER + 2 * SSM_GROUPS * SSM_STATE
DT_WIDTH = 2 * SSM_HEADS
N_MEM = 256
MEM_HEADS = 4
MEM_HEAD_DIM = 128
MEM_WIDTH = MEM_HEADS * MEM_HEAD_DIM
D_FF = 2816
FFN_CONV = 3
EPS = 1e-6

LANES = 128
QK_WIDTH = ATTN_WIDTH + KV_WIDTH
HEADS_PER_STATE_GROUP = SSM_HEADS // SSM_GROUPS
GROUP_INNER = HEADS_PER_STATE_GROUP * SSM_HEAD_DIM
NEG_BIG = -1e30
VMEM_LIMIT = 56 * 1024 * 1024

TM_PROJ = 512
TQ_ATTN = 512
TC_CONV = 512
TM_OUT = 512
TM_FFN = 512
FF_CHUNK = 256
HALO = 8
HALO_BF16 = 16


def _dot(a, b):
    return jnp.dot(a, b, preferred_element_type=F32)


def _dot_nt(a, b):
    return lax.dot_general(a, b, (((1,), (1,)), ((), ())), preferred_element_type=F32)


def _dot_tn(a, b):
    return lax.dot_general(a, b, (((0,), (0,)), ((), ())), preferred_element_type=F32)


def _split3(a):
    hi = a.astype(BF16)
    r = a - hi.astype(F32)
    mid = r.astype(BF16)
    lo = (r - mid.astype(F32)).astype(BF16)
    return hi, mid, lo


def _silu(x):
    return x / (1.0 + jnp.exp(-x))


def _softplus(x):
    return jnp.maximum(x, 0.0) + jnp.log1p(jnp.exp(-jnp.abs(x)))


def _params(sem):
    return pltpu.CompilerParams(dimension_semantics=sem, vmem_limit_bytes=VMEM_LIMIT)


def _const_spec(shape):
    nd = len(shape)
    return pl.BlockSpec(shape, lambda *_: (0,) * nd)


def _in_proj_kernel(x_ref, n1_ref, wqk_ref, wv_ref, wz_ref, wx_ref, wdt_ref, wdtt_ref, wmq_ref,
                    hsum_ref, hexp_ref, qkg_ref, mqg_ref, cos_ref, sin_ref,
                    q_ref, k_ref, v_ref, z_ref, xbc_ref, dt_ref, dtt_ref, mq_ref):
    x = x_ref[...]
    ms = jnp.mean(x * x, axis=-1, keepdims=True)
    h = (x * lax.rsqrt(ms + EPS) * n1_ref[...]).astype(BF16)

    qk = _dot(h, wqk_ref[...])
    ssq = _dot((qk * qk).astype(BF16), hsum_ref[...])
    inv = lax.rsqrt(ssq * (1.0 / HEAD_DIM) + EPS)
    inv_hi = inv.astype(BF16)
    inv_lo = (inv - inv_hi.astype(F32)).astype(BF16)
    scale = _dot(jnp.concatenate([inv_hi, inv_lo], axis=1), hexp_ref[...])
    qkn = qk * scale * qkg_ref[...]

    cos = cos_ref[...]
    sin = sin_ref[...]
    for p in range(GQA_GROUP):
        u = qkn[:, p * 256:p * 256 + LANES]
        w = qkn[:, p * 256 + LANES:(p + 1) * 256]
        q_ref[:, p * 256:p * 256 + LANES] = (u * cos - w * sin).astype(BF16)
        q_ref[:, p * 256 + LANES:(p + 1) * 256] = (w * cos + u * sin).astype(BF16)
    u = qkn[:, ATTN_WIDTH:ATTN_WIDTH + LANES]
    w = qkn[:, ATTN_WIDTH + LANES:QK_WIDTH]
    k_ref[:, :LANES] = (u * cos - w * sin).astype(BF16)
    k_ref[:, LANES:] = (w * cos + u * sin).astype(BF16)

    v_ref[...] = _dot(h, wv_ref[...]).astype(BF16)
    z_ref[...] = _dot(h, wz_ref[...]).astype(BF16)
    xbc_ref[...] = _dot(h, wx_ref[...]).astype(BF16)
    dt_ref[...] = _dot(h, wdt_ref[...])[:, :DT_WIDTH]
    dtt_ref[...] = _dot_nt(wdtt_ref[...], h)

    mq = _dot(h, wmq_ref[...])
    mqg = mqg_ref[...] * (MEM_HEAD_DIM ** -0.5)
    for hh in range(MEM_HEADS):
        m = mq[:, hh * LANES:(hh + 1) * LANES]
        r = lax.rsqrt(jnp.mean(m * m, axis=-1, keepdims=True) + EPS)
        mq_ref[:, hh * LANES:(hh + 1) * LANES] = (m * r * mqg).astype(BF16)


def _in_proj(x2d, s, w):
    t = x2d.shape[0]
    tm = TM_PROJ
    nblk_seq = s // tm
    row = lambda i: (i, 0)
    tab = lambda i: (i % nblk_seq, 0)
    in_specs = [
        pl.BlockSpec((tm, D_MODEL), row),
        _const_spec((1, D_MODEL)),
        _const_spec((D_MODEL, QK_WIDTH)),
        _const_spec((D_MODEL, KV_WIDTH)),
        _const_spec((D_MODEL, SSM_INNER)),
        _const_spec((D_MODEL, XBC_WIDTH)),
        _const_spec((D_MODEL, LANES)),
        _const_spec((DT_WIDTH, D_MODEL)),
        _const_spec((D_MODEL, MEM_WIDTH)),
        _const_spec((QK_WIDTH, LANES)),
        _const_spec((2 * LANES, QK_WIDTH)),
        _const_spec((1, QK_WIDTH)),
        _const_spec((1, LANES)),
        pl.BlockSpec((tm, LANES), tab),
        pl.BlockSpec((tm, LANES), tab),
    ]
    out_shape = [
        jax.ShapeDtypeStruct((t, ATTN_WIDTH), BF16),
        jax.ShapeDtypeStruct((t, KV_WIDTH), BF16),
        jax.ShapeDtypeStruct((t, KV_WIDTH), BF16),
        jax.ShapeDtypeStruct((t, SSM_INNER), BF16),
        jax.ShapeDtypeStruct((t, XBC_WIDTH), BF16),
        jax.ShapeDtypeStruct((t, DT_WIDTH), F32),
        jax.ShapeDtypeStruct((DT_WIDTH, t), F32),
        jax.ShapeDtypeStruct((t, MEM_WIDTH), BF16),
    ]
    out_specs = [
        pl.BlockSpec((tm, ATTN_WIDTH), row),
        pl.BlockSpec((tm, KV_WIDTH), row),
        pl.BlockSpec((tm, KV_WIDTH), row),
        pl.BlockSpec((tm, SSM_INNER), row),
        pl.BlockSpec((tm, XBC_WIDTH), row),
        pl.BlockSpec((tm, DT_WIDTH), row),
        pl.BlockSpec((DT_WIDTH, tm), lambda i: (0, i)),
        pl.BlockSpec((tm, MEM_WIDTH), row),
    ]
    return pl.pallas_call(
        _in_proj_kernel, grid=(t // tm,), in_specs=in_specs, out_specs=out_specs, out_shape=out_shape,
        compiler_params=_params(("parallel",)), name="in_proj",
    )(x2d, w["n1"], w["wqk"], w["wv"], w["wz"], w["wx"], w["wdt"], w["wdtt"], w["wmq"],
      w["hsum"], w["hexp"], w["qkg"], w["mqg"], w["cos"][:s], w["sin"][:s])


def _mem_kv_kernel(mem_ref, nw_ref, w_ref, mkg_ref, mk_ref, mv_ref):
    x = mem_ref[0]
    ms = jnp.mean(x * x, axis=-1, keepdims=True)
    h = (x * lax.rsqrt(ms + EPS) * nw_ref[...]).astype(BF16)
    kv = _dot(h, w_ref[...])
    g = mkg_ref[...]
    for hh in range(MEM_HEADS):
        m = kv[:, hh * LANES:(hh + 1) * LANES]
        r = lax.rsqrt(jnp.mean(m * m, axis=-1, keepdims=True) + EPS)
        mk_ref[0, :, hh * LANES:(hh + 1) * LANES] = (m * r * g).astype(BF16)
    mv_ref[0] = kv[:, MEM_WIDTH:].astype(BF16)


def _mem_kv(mem, w):
    b = mem.shape[0]
    blk = lambda i: (i, 0, 0)
    return pl.pallas_call(
        _mem_kv_kernel, grid=(b,),
        in_specs=[pl.BlockSpec((1, N_MEM, D_MODEL), blk), _const_spec((1, D_MODEL)),
                  _const_spec((D_MODEL, 2 * MEM_WIDTH)), _const_spec((1, LANES))],
        out_specs=[pl.BlockSpec((1, N_MEM, MEM_WIDTH), blk), pl.BlockSpec((1, N_MEM, MEM_WIDTH), blk)],
        out_shape=[jax.ShapeDtypeStruct((b, N_MEM, MEM_WIDTH), BF16)] * 2,
        compiler_params=_params(("parallel",)), name="mem_kv",
    )(mem, w["memn"], w["wmemkv"], w["mkg"])


def _attention_kernel(sink_ref, q_ref, k_ref, v_ref, mq_ref, mk_ref, mv_ref, o_ref, mo_ref, *, seq):
    qi = pl.program_id(1)
    tq = q_ref.shape[1]
    kwin = 3 * BLOCK

    lane_k = lax.broadcasted_iota(jnp.int32, (kwin, 2 * LANES), 1)
    lane_o = lax.broadcasted_iota(jnp.int32, (BLOCK, 2 * LANES), 1)
    row_i = lax.broadcasted_iota(jnp.int32, (BLOCK, kwin), 0)
    col_i = lax.broadcasted_iota(jnp.int32, (BLOCK, kwin), 1)

    def block_body(blk, carry):
        r0 = qi * tq + blk * BLOCK
        ks = pl.multiple_of(jnp.clip(r0 - BLOCK, 0, seq - kwin), BLOCK)
        rq = pl.multiple_of(blk * BLOCK, BLOCK)
        kw = k_ref[0, pl.ds(ks, kwin), :]
        vw = v_ref[0, pl.ds(ks, kwin), :]
        delta = (col_i - row_i) + (ks - r0)
        bias = jnp.where(jnp.abs(delta) <= WINDOW, 0.0, NEG_BIG).astype(F32)
        zero = jnp.zeros_like(kw)
        kj = [jnp.where((lane_k % LANES) // HALF == j, kw, zero) for j in range(N_KV_HEADS)]
        v4 = jnp.concatenate([jnp.where(lane_k // HEAD_DIM == j, vw, zero) for j in range(N_KV_HEADS)], axis=0)
        for p in range(GQA_GROUP):
            qp = q_ref[0, pl.ds(rq, BLOCK), p * 256:(p + 1) * 256]
            probs = []
            invs = []
            for j in range(N_KV_HEADS):
                snk = sink_ref[j * GQA_GROUP + p]
                sc = _dot_nt(qp, kj[j]) + bias
                m = jnp.maximum(jnp.max(sc, axis=-1, keepdims=True), snk)
                e = jnp.exp(sc - m)
                den = jnp.sum(e, axis=-1, keepdims=True) + jnp.exp(snk - m)
                probs.append(e.astype(BF16))
                invs.append(1.0 / den)
            o = _dot(jnp.concatenate(probs, axis=1), v4)
            inv = jnp.where(lane_o < HEAD_DIM, invs[0],
                            jnp.where(lane_o < 2 * HEAD_DIM, invs[1],
                                      jnp.where(lane_o < 3 * HEAD_DIM, invs[2], invs[3])))
            o_ref[0, pl.ds(rq, BLOCK), p * 256:(p + 1) * 256] = (o * inv).astype(BF16)
        return carry

    lax.fori_loop(0, tq // BLOCK, block_body, 0)

    for hh in range(MEM_HEADS):
        qm = mq_ref[0, :, hh * LANES:(hh + 1) * LANES]
        km = mk_ref[0, :, hh * LANES:(hh + 1) * LANES]
        vm = mv_ref[0, :, hh * LANES:(hh + 1) * LANES]
        sc = _dot_nt(qm, km)
        m = jnp.max(sc, axis=-1, keepdims=True)
        e = jnp.exp(sc - m)
        den = jnp.sum(e, axis=-1, keepdims=True)
        o = _dot(e.astype(BF16), vm)
        mo_ref[0, :, hh * LANES:(hh + 1) * LANES] = (o * (1.0 / den)).astype(BF16)


def _attention(q, k, v, mq, mk, mv, sink):
    b, s, _ = q.shape
    tq = TQ_ATTN
    qblk = lambda bi, i: (bi, i, 0)
    full = lambda bi, i: (bi, 0, 0)
    return pl.pallas_call(
        functools.partial(_attention_kernel, seq=s), grid=(b, s // tq),
        in_specs=[pl.BlockSpec(memory_space=pltpu.SMEM),
                  pl.BlockSpec((1, tq, ATTN_WIDTH), qblk),
                  pl.BlockSpec((1, s, KV_WIDTH), full),
                  pl.BlockSpec((1, s, KV_WIDTH), full),
                  pl.BlockSpec((1, tq, MEM_WIDTH), qblk),
                  pl.BlockSpec((1, N_MEM, MEM_WIDTH), full),
                  pl.BlockSpec((1, N_MEM, MEM_WIDTH), full)],
        out_specs=[pl.BlockSpec((1, tq, ATTN_WIDTH), qblk), pl.BlockSpec((1, tq, MEM_WIDTH), qblk)],
        out_shape=[jax.ShapeDtypeStruct((b, s, ATTN_WIDTH), BF16), jax.ShapeDtypeStruct((b, s, MEM_WIDTH), BF16)],
        compiler_params=_params(("parallel", "arbitrary")), name="attention",
    )(sink, q, k, v, mq, mk, mv)


def _ssd_conv_kernel(x_ref, prev_ref, next_ref, w_ref, b_ref, o_ref, buf_ref):
    ci = pl.program_id(1)
    nci = pl.num_programs(1)
    tc = x_ref.shape[1]
    hb = HALO_BF16
    pad = SSM_CONV // 2
    buf_ref[0:hb, :] = prev_ref[0].astype(F32) * (ci > 0).astype(F32)
    buf_ref[hb:hb + tc, :] = x_ref[0].astype(F32)
    buf_ref[hb + tc:, :] = next_ref[0].astype(F32) * (ci < nci - 1).astype(F32)
    y = b_ref[...] + buf_ref[pl.ds(hb - pad, tc), :] * w_ref[0:1, :]
    for t in range(1, SSM_CONV):
        y = y + buf_ref[pl.ds(hb - pad + t, tc), :] * w_ref[t:t + 1, :]
    o_ref[0] = _silu(y).astype(BF16)


def _ssd_conv(xbc, w):
    b, s, _ = xbc.shape
    tc = TC_CONV
    r = tc // HALO_BF16
    nh = s // HALO_BF16
    return pl.pallas_call(
        _ssd_conv_kernel, grid=(b, s // tc),
        in_specs=[pl.BlockSpec((1, tc, XBC_WIDTH), lambda bi, i: (bi, i, 0)),
                  pl.BlockSpec((1, HALO_BF16, XBC_WIDTH), lambda bi, i: (bi, jnp.maximum(i * r - 1, 0), 0)),
                  pl.BlockSpec((1, HALO_BF16, XBC_WIDTH), lambda bi, i: (bi, jnp.minimum((i + 1) * r, nh - 1), 0)),
                  _const_spec((SSM_CONV, XBC_WIDTH)), _const_spec((1, XBC_WIDTH))],
        out_specs=pl.BlockSpec((1, tc, XBC_WIDTH), lambda bi, i: (bi, i, 0)),
        out_shape=jax.ShapeDtypeStruct((b, s, XBC_WIDTH), BF16),
        scratch_shapes=[pltpu.VMEM((tc + 2 * HALO_BF16, XBC_WIDTH), F32)],
        compiler_params=_params(("parallel", "parallel")), name="ssd_conv",
    )(xbc, xbc, xbc, w["convw"], w["convb"])


def _slab(cols, first):
    lane = lax.broadcasted_iota(jnp.int32, (SSM_CHUNK, GROUP_INNER), 1)
    c = [cols[:, first + e:first + e + 1] for e in range(HEADS_PER_STATE_GROUP)]
    return jnp.where(lane < 64, c[0], jnp.where(lane < 128, c[1], jnp.where(lane < 192, c[2], c[3])))


def _slab_row(row, first):
    lane = lax.broadcasted_iota(jnp.int32, (1, GROUP_INNER), 1)
    c = [row[:, first + e:first + e + 1] for e in range(HEADS_PER_STATE_GROUP)]
    return jnp.where(lane < 64, c[0], jnp.where(lane < 128, c[1], jnp.where(lane < 192, c[2], c[3])))


def _ssd_scan_kernel(xf_ref, xb_ref, dtf_ref, dtb_ref, dttf_ref, bias_r_ref, bias_c_ref,
                     alog_r_ref, alog_c_ref, dskip_ref, ya_ref, yb_ref, hf_ref, hb_ref):
    c = pl.program_id(1)

    @pl.when(c == 0)
    def _():
        hf_ref[...] = jnp.zeros_like(hf_ref)
        hb_ref[...] = jnp.zeros_like(hb_ref)

    ln = SSM_CHUNK
    ri = lax.broadcasted_iota(jnp.int32, (ln, ln), 0)
    cj = lax.broadcasted_iota(jnp.int32, (ln, ln), 1)
    low_incl = (cj <= ri)
    tri_l = jnp.where(low_incl, 1.0, 0.0).astype(BF16)
    tri_u = jnp.where(cj >= ri, 1.0, 0.0).astype(BF16)

    a_row = -jnp.exp(alog_r_ref[...])
    a_col = -jnp.exp(alog_c_ref[...])

    def exact_left(m, a):
        hi, mid, lo = _split3(a)
        return _dot(m, hi) + _dot(m, mid) + _dot(m, lo)

    def exact_right(a, m):
        hi, mid, lo = _split3(a)
        return _dot(hi, m) + _dot(mid, m) + _dot(lo, m)

    xc = xf_ref[0]
    xs = xc[:, :SSM_INNER]
    dt_c = _softplus(dtf_ref[0] + bias_r_ref[...])
    dt_r = _softplus(dttf_ref[...] + bias_c_ref[...])
    a_c = dt_c * a_row
    a_r = dt_r * a_col
    cs_c = exact_left(tri_l, a_c)
    sf_c = exact_left(tri_u, a_c)
    cs_r = exact_right(a_r, tri_u)
    sf_r = exact_right(a_r, tri_l)

    lt = cj < ri
    gt = cj > ri
    y_parts = []
    for g in range(SSM_GROUPS):
        bm = xc[:, SSM_INNER + g * SSM_STATE:SSM_INNER + (g + 1) * SSM_STATE]
        cm = xc[:, SSM_INNER + SSM_GROUPS * SSM_STATE + g * SSM_STATE:SSM_INNER + SSM_GROUPS * SSM_STATE + (g + 1) * SSM_STATE]
        cb = _dot_nt(cm, bm)
        for pair in range(HEADS_PER_STATE_GROUP // 2):
            ms = []
            for e in range(2):
                hd = g * HEADS_PER_STATE_GROUP + pair * 2 + e
                hb_i = SSM_HEADS + hd
                dl = cs_c[:, hd:hd + 1] - cs_r[hd:hd + 1, :]
                du = sf_c[:, hb_i:hb_i + 1] - sf_r[hb_i:hb_i + 1, :]
                ex = jnp.exp(jnp.where(low_incl, dl, du))
                dtf_j = dt_r[hd:hd + 1, :]
                dtb_j = dt_r[hb_i:hb_i + 1, :]
                dts = jnp.where(lt, dtf_j, jnp.where(gt, dtb_j, dtf_j + dtb_j))
                ms.append((cb * ex * dts).astype(BF16))
            col0 = (g * HEADS_PER_STATE_GROUP + pair * 2) * SSM_HEAD_DIM
            xp = xs[:, col0:col0 + LANES]
            lane = lax.broadcasted_iota(jnp.int32, (ln, LANES), 1)
            zero = jnp.zeros_like(xp)
            xbd = jnp.concatenate([jnp.where(lane < SSM_HEAD_DIM, xp, zero),
                                   jnp.where(lane >= SSM_HEAD_DIM, xp, zero)], axis=0)
            y_parts.append(_dot(jnp.concatenate(ms, axis=1), xbd))
    y = jnp.concatenate(y_parts, axis=1)
    xs32 = xs.astype(F32)
    y = y + xs32 * dskip_ref[...]

    e_cs = jnp.exp(cs_c)
    last = cs_c[ln - 1:ln, :]
    w_state = dt_c * jnp.exp(last - cs_c)
    e_last = jnp.exp(last)
    y_off = []
    for g in range(SSM_GROUPS):
        first = g * HEADS_PER_STATE_GROUP
        bm = xc[:, SSM_INNER + g * SSM_STATE:SSM_INNER + (g + 1) * SSM_STATE]
        cm = xc[:, SSM_INNER + SSM_GROUPS * SSM_STATE + g * SSM_STATE:SSM_INNER + SSM_GROUPS * SSM_STATE + (g + 1) * SSM_STATE]
        hprev = hf_ref[g]
        y_off.append(_dot(cm, hprev.astype(BF16)) * _slab(e_cs, first))
        xw = (xs32[:, g * GROUP_INNER:(g + 1) * GROUP_INNER] * _slab(w_state, first)).astype(BF16)
        hf_ref[g] = hprev * _slab_row(e_last, first) + _dot_tn(bm, xw)
    ya_ref[0] = y + jnp.concatenate(y_off, axis=1)

    xc = xb_ref[0]
    xs32 = xc[:, :SSM_INNER].astype(F32)
    dt_c = _softplus(dtb_ref[0] + bias_r_ref[...])
    a_c = dt_c * a_row
    sf_c = exact_left(tri_u, a_c)
    e_sf = jnp.exp(sf_c)
    head0 = sf_c[0:1, :]
    w_state = dt_c * jnp.exp(head0 - sf_c)
    e_head = jnp.exp(head0)
    y_off = []
    for g in range(SSM_GROUPS):
        first = SSM_HEADS + g * HEADS_PER_STATE_GROUP
        bm = xc[:, SSM_INNER + g * SSM_STATE:SSM_INNER + (g + 1) * SSM_STATE]
        cm = xc[:, SSM_INNER + SSM_GROUPS * SSM_STATE + g * SSM_STATE:SSM_INNER + SSM_GROUPS * SSM_STATE + (g + 1) * SSM_STATE]
        hprev = hb_ref[g]
        y_off.append(_dot(cm, hprev.astype(BF16)) * _slab(e_sf, first))
        xw = (xs32[:, g * GROUP_INNER:(g + 1) * GROUP_INNER] * _slab(w_state, first)).astype(BF16)
        hb_ref[g] = hprev * _slab_row(e_head, first) + _dot_tn(bm, xw)
    yb_ref[0] = jnp.concatenate(y_off, axis=1)


def _ssd_scan(xc, dt, dtt, w):
    b, s, _ = xc.shape
    nc = s // SSM_CHUNK
    fwd = lambda bi, c: (bi, c, 0)
    bwd = lambda bi, c: (bi, nc - 1 - c, 0)
    return pl.pallas_call(
        _ssd_scan_kernel, grid=(b, nc),
        in_specs=[pl.BlockSpec((1, SSM_CHUNK, XBC_WIDTH), fwd),
                  pl.BlockSpec((1, SSM_CHUNK, XBC_WIDTH), bwd),
                  pl.BlockSpec((1, SSM_CHUNK, DT_WIDTH), fwd),
                  pl.BlockSpec((1, SSM_CHUNK, DT_WIDTH), bwd),
                  pl.BlockSpec((DT_WIDTH, SSM_CHUNK), lambda bi, c: (0, bi * nc + c)),
                  _const_spec((1, DT_WIDTH)), _const_spec((DT_WIDTH, 1)),
                  _const_spec((1, DT_WIDTH)), _const_spec((DT_WIDTH, 1)),
                  _const_spec((1, SSM_INNER))],
        out_specs=[pl.BlockSpec((1, SSM_CHUNK, SSM_INNER), fwd), pl.BlockSpec((1, SSM_CHUNK, SSM_INNER), bwd)],
        out_shape=[jax.ShapeDtypeStruct((b, s, SSM_INNER), F32)] * 2,
        scratch_shapes=[pltpu.VMEM((SSM_GROUPS, SSM_STATE, GROUP_INNER), F32)] * 2,
        compiler_params=_params(("parallel", "arbitrary")), name="ssd_scan",
    )(xc, xc, dt, dt, dtt, w["dtb_r"], w["dtb_c"], w["alog_r"], w["alog_c"], w["dskip"])


def _out_proj_kernel(x_ref, attn_ref, ya_ref, yb_ref, z_ref, mem_ref, sn_ref, wa_ref, ws_ref, wm_ref, o_ref):
    y = (ya_ref[...] + yb_ref[...]) * _silu(z_ref[...].astype(F32))
    ms = jnp.mean(y * y, axis=-1, keepdims=True)
    ssm = (y * lax.rsqrt(ms + EPS) * sn_ref[...]).astype(BF16)
    acc = _dot(attn_ref[...], wa_ref[...]) + _dot(ssm, ws_ref[...]) + _dot(mem_ref[...], wm_ref[...])
    o_ref[...] = x_ref[...] + acc


def _out_proj(x2d, attn, ya, yb, z, memo, w):
    t = x2d.shape[0]
    tm = TM_OUT
    row = lambda i: (i, 0)
    return pl.pallas_call(
        _out_proj_kernel, grid=(t // tm,),
        in_specs=[pl.BlockSpec((tm, D_MODEL), row), pl.BlockSpec((tm, ATTN_WIDTH), row),
                  pl.BlockSpec((tm, SSM_INNER), row), pl.BlockSpec((tm, SSM_INNER), row),
                  pl.BlockSpec((tm, SSM_INNER), row), pl.BlockSpec((tm, MEM_WIDTH), row),
                  _const_spec((1, SSM_INNER)), _const_spec((ATTN_WIDTH, D_MODEL)),
                  _const_spec((SSM_INNER, D_MODEL)), _const_spec((MEM_WIDTH, D_MODEL))],
        out_specs=pl.BlockSpec((tm, D_MODEL), row),
        out_shape=jax.ShapeDtypeStruct((t, D_MODEL), F32),
        compiler_params=_params(("parallel",)), name="out_proj",
    )(x2d, attn, ya, yb, z, memo, w["ssmn"], w["wo_a"], w["wo_s"], w["wo_m"])


def _ffn_kernel(x_ref, prev_ref, next_ref, n2_ref, wg_ref, wu_ref, wd_ref, cw_ref, cb_ref, o_ref,
                h_ref, g_ref, u_ref, acc_ref):
    i = pl.program_id(1)
    ni = pl.num_programs(1)
    tm = x_ref.shape[1]
    n2 = n2_ref[...]

    def norm(v):
        ms = jnp.mean(v * v, axis=-1, keepdims=True)
        return (v * lax.rsqrt(ms + EPS) * n2).astype(BF16)

    keep_prev = (i > 0).astype(F32)
    keep_next = (i < ni - 1).astype(F32)
    hl = HALO_BF16
    h_ref[0:hl, :] = norm(prev_ref[0])
    h_ref[hl:hl + tm, :] = norm(x_ref[0])
    h_ref[hl + tm:, :] = norm(next_ref[0])
    acc_ref[...] = jnp.zeros_like(acc_ref)
    pad = FFN_CONV // 2

    def chunk(cix, carry):
        c0 = pl.multiple_of(cix * FF_CHUNK, FF_CHUNK)
        h = h_ref[...]
        for dst, wref in ((g_ref, wg_ref), (u_ref, wu_ref)):
            r = _dot(h, wref[:, pl.ds(c0, FF_CHUNK)])
            dst[0:hl, :] = r[0:hl, :] * keep_prev
            dst[hl:hl + tm, :] = r[hl:hl + tm, :]
            dst[hl + tm:, :] = r[hl + tm:, :] * keep_next
        outs = []
        for src, off in ((g_ref, 0), (u_ref, D_FF)):
            cw = cw_ref[:, pl.ds(off + c0, FF_CHUNK)]
            cb = cb_ref[:, pl.ds(off + c0, FF_CHUNK)]
            y = cb + src[pl.ds(hl - pad, tm), :] * cw[0:1, :]
            for t in range(1, FFN_CONV):
                y = y + src[pl.ds(hl - pad + t, tm), :] * cw[t:t + 1, :]
            outs.append(y)
        act = (_silu(outs[0]) * outs[1]).astype(BF16)
        acc_ref[...] += _dot(act, wd_ref[pl.ds(c0, FF_CHUNK), :])
        return carry

    lax.fori_loop(0, D_FF // FF_CHUNK, chunk, 0)
    o_ref[0] = x_ref[0] + acc_ref[...]


def _ffn(x1, w):
    b, s, _ = x1.shape
    tm = TM_FFN
    hl = HALO_BF16
    r = tm // hl
    nh = s // hl
    return pl.pallas_call(
        _ffn_kernel, grid=(b, s // tm),
        in_specs=[pl.BlockSpec((1, tm, D_MODEL), lambda bi, i: (bi, i, 0)),
                  pl.BlockSpec((1, hl, D_MODEL), lambda bi, i: (bi, jnp.maximum(i * r - 1, 0), 0)),
                  pl.BlockSpec((1, hl, D_MODEL), lambda bi, i: (bi, jnp.minimum((i + 1) * r, nh - 1), 0)),
                  _const_spec((1, D_MODEL)),
                  _const_spec((D_MODEL, D_FF)), _const_spec((D_MODEL, D_FF)), _const_spec((D_FF, D_MODEL)),
                  _const_spec((FFN_CONV, 2 * D_FF)), _const_spec((1, 2 * D_FF))],
        out_specs=pl.BlockSpec((1, tm, D_MODEL), lambda bi, i: (bi, i, 0)),
        out_shape=jax.ShapeDtypeStruct((b, s, D_MODEL), F32),
        scratch_shapes=[pltpu.VMEM((tm + 2 * hl, D_MODEL), BF16),
                        pltpu.VMEM((tm + 2 * hl, FF_CHUNK), F32),
                        pltpu.VMEM((tm + 2 * hl, FF_CHUNK), F32),
                        pltpu.VMEM((tm, D_MODEL), F32)],
        compiler_params=_params(("parallel", "parallel")), name="ffn",
    )(x1, x1, x1, w["n2"], w["wg"], w["wu"], w["wd"], w["fcw"], w["fcb"])


def _prep_weights(max_seq, norm1_w, w_in, q_norm_w, k_norm_w, attn_sink, ssm_conv_w, ssm_conv_b, ssm_dt_bias,
                  ssm_A_log, ssm_D, ssm_norm_w, mem_norm_w, w_mem_kv, mq_norm_w, mk_norm_w, w_out, norm2_w,
                  w_ffn_up, ffn_conv_w, ffn_conv_b, w_ffn_down):
    o = 0
    wq = w_in[:, o:o + ATTN_WIDTH]; o += ATTN_WIDTH
    wk = w_in[:, o:o + KV_WIDTH]; o += KV_WIDTH
    wv = w_in[:, o:o + KV_WIDTH]; o += KV_WIDTH
    wz = w_in[:, o:o + SSM_INNER]; o += SSM_INNER
    wx = w_in[:, o:o + XBC_WIDTH]; o += XBC_WIDTH
    wdt = w_in[:, o:o + DT_WIDTH]; o += DT_WIDTH
    wmq = w_in[:, o:o + MEM_WIDTH]

    wq_p = wq.reshape(D_MODEL, N_KV_HEADS, GQA_GROUP, 2, HALF).transpose(0, 2, 3, 1, 4).reshape(D_MODEL, ATTN_WIDTH)
    wk_p = wk.reshape(D_MODEL, N_KV_HEADS, 2, HALF).transpose(0, 2, 1, 3).reshape(D_MODEL, KV_WIDTH)
    qg = jnp.broadcast_to(q_norm_w.reshape(1, 2, 1, HALF), (GQA_GROUP, 2, N_KV_HEADS, HALF)).reshape(ATTN_WIDTH)
    kg = jnp.broadcast_to(k_norm_w.reshape(2, 1, HALF), (2, N_KV_HEADS, HALF)).reshape(KV_WIDTH)
    qkg = jnp.concatenate([qg * (HEAD_DIM ** -0.5), kg]).reshape(1, QK_WIDTH)

    cq = np.arange(ATTN_WIDTH)
    eq = (cq // 256) * N_KV_HEADS + (cq % LANES) // HALF
    ck = np.arange(KV_WIDTH)
    ek = N_Q_HEADS + (ck % LANES) // HALF
    e_all = np.concatenate([eq, ek])
    hsum = np.zeros((QK_WIDTH, LANES), np.float32)
    hsum[np.arange(QK_WIDTH), e_all] = 1.0
    hexp = np.concatenate([hsum.T, hsum.T], axis=0)

    inv = ROPE_THETA ** (-jnp.arange(0, HEAD_DIM, 2, dtype=F32) / HEAD_DIM)
    ang = jnp.arange(max_seq, dtype=F32)[:, None] * inv[None, :]
    cos = jnp.tile(jnp.cos(ang), (1, LANES // HALF))
    sin = jnp.tile(jnp.sin(ang), (1, LANES // HALF))

    wo_a = w_out[:ATTN_WIDTH].reshape(N_KV_HEADS, GQA_GROUP, HEAD_DIM, D_MODEL).transpose(1, 0, 2, 3)
    wo_a = wo_a.reshape(ATTN_WIDTH, D_MODEL)

    wdt_pad = jnp.zeros((D_MODEL, LANES), F32).at[:, :DT_WIDTH].set(wdt)
    return {
        "n1": norm1_w.reshape(1, D_MODEL),
        "wqk": jnp.concatenate([wq_p, wk_p], axis=1).astype(BF16),
        "wv": wv.astype(BF16), "wz": wz.astype(BF16), "wx": wx.astype(BF16),
        "wdt": wdt_pad.astype(BF16), "wdtt": wdt.T.astype(BF16), "wmq": wmq.astype(BF16),
        "hsum": jnp.asarray(hsum, BF16), "hexp": jnp.asarray(hexp, BF16),
        "qkg": qkg, "mqg": mq_norm_w.reshape(1, LANES), "cos": cos, "sin": sin,
        "sink": attn_sink,
        "memn": mem_norm_w.reshape(1, D_MODEL), "wmemkv": w_mem_kv.astype(BF16), "mkg": mk_norm_w.reshape(1, LANES),
        "convw": ssm_conv_w, "convb": ssm_conv_b.reshape(1, XBC_WIDTH),
        "dtb_r": ssm_dt_bias.reshape(1, DT_WIDTH), "dtb_c": ssm_dt_bias.reshape(DT_WIDTH, 1),
        "alog_r": ssm_A_log.reshape(1, DT_WIDTH), "alog_c": ssm_A_log.reshape(DT_WIDTH, 1),
        "dskip": jnp.repeat(ssm_D, SSM_HEAD_DIM).reshape(1, SSM_INNER),
        "ssmn": ssm_norm_w.reshape(1, SSM_INNER),
        "wo_a": wo_a.astype(BF16),
        "wo_s": w_out[ATTN_WIDTH:ATTN_WIDTH + SSM_INNER].astype(BF16),
        "wo_m": w_out[ATTN_WIDTH + SSM_INNER:].astype(BF16),
        "n2": norm2_w.reshape(1, D_MODEL),
        "wg": w_ffn_up[:, :D_FF].astype(BF16), "wu": w_ffn_up[:, D_FF:].astype(BF16),
        "wd": w_ffn_down.astype(BF16),
        "fcw": ffn_conv_w, "fcb": ffn_conv_b.reshape(1, 2 * D_FF),
    }


def _encoder_layer(x, mem, w):
    b, s, _ = x.shape
    t = b * s
    x2d = x.reshape(t, D_MODEL)
    q, k, v, z, xbc, dt, dtt, mq = _in_proj(x2d, s, w)
    mk, mv = _mem_kv(mem, w)
    attn, memo = _attention(q.reshape(b, s, -1), k.reshape(b, s, -1), v.reshape(b, s, -1),
                            mq.reshape(b, s, -1), mk, mv, w["sink"])
    xc = _ssd_conv(xbc.reshape(b, s, -1), w)
    ya, yb = _ssd_scan(xc, dt.reshape(b, s, -1), dtt, w)
    x1 = _out_proj(x2d, attn.reshape(t, -1), ya.reshape(t, -1), yb.reshape(t, -1), z, memo.reshape(t, -1), w)
    return _ffn(x1.reshape(b, s, D_MODEL), w)


def kernel(x_prompt, x_sample, mem_prompt, mem_sample, norm1_w, w_in, q_norm_w, k_norm_w, attn_sink, ssm_conv_w, ssm_conv_b, ssm_dt_bias, ssm_A_log, ssm_D, ssm_norm_w, mem_norm_w, w_mem_kv, mq_norm_w, mk_norm_w, w_out, norm2_w, w_ffn_up, ffn_conv_w, ffn_conv_b, w_ffn_down):
    weights = (norm1_w, w_in, q_norm_w, k_norm_w, attn_sink, ssm_conv_w, ssm_conv_b, ssm_dt_bias,
               ssm_A_log, ssm_D, ssm_norm_w, mem_norm_w, w_mem_kv, mq_norm_w, mk_norm_w, w_out,
               norm2_w, w_ffn_up, ffn_conv_w, ffn_conv_b, w_ffn_down)
    depth = norm1_w.shape[0]
    max_seq = max(x_prompt.shape[1], x_sample.shape[1])
    y_prompt, y_sample = x_prompt, x_sample
    for layer in range(depth):
        w = _prep_weights(max_seq, *[p[layer] for p in weights])
        y_prompt = _encoder_layer(y_prompt, mem_prompt, w)
        y_sample = _encoder_layer(y_sample, mem_sample, w)
    return (y_prompt, y_sample)
```

```python
import functools
import math

import numpy as np
import jax
import jax.numpy as jnp
from jax import lax
from jax.experimental import pallas as pl
from jax.experimental.pallas import tpu as pltpu

F32 = jnp.float32
BF16 = jnp.bfloat16

D_MODEL = 1024
HEAD_DIM = 64
HALF = HEAD_DIM // 2
N_Q_HEADS = 16
N_KV_HEADS = 4
GQA_GROUP = N_Q_HEADS // N_KV_HEADS
ATTN_WIDTH = N_Q_HEADS * HEAD_DIM
KV_WIDTH = N_KV_HEADS * HEAD_DIM
WINDOW = 128
BLOCK = 128
ROPE_THETA = 10000.0
SSM_HEADS = 8
SSM_HEAD_DIM = 64
SSM_INNER = SSM_HEADS * SSM_HEAD_DIM
SSM_GROUPS = 2
SSM_STATE = 128
SSM_CONV = 5
SSM_CHUNK = 128
XBC_WIDTH = SSM_INNER + 2 * SSM_GROUPS * SSM_STATE
DT_WIDTH = 2 * SSM_HEADS
N_MEM = 256
MEM_HEADS = 4
MEM_HEAD_DIM = 128
MEM_WIDTH = MEM_HEADS * MEM_HEAD_DIM
D_FF = 2816
FFN_CONV = 3
EPS = 1e-6

LANES = 128
QK_WIDTH = ATTN_WIDTH + KV_WIDTH
HEADS_PER_STATE_GROUP = SSM_HEADS // SSM_GROUPS
GROUP_INNER = HEADS_PER_STATE_GROUP * SSM_HEAD_DIM
NEG_BIG = -1e30
LOG2E = math.log2(math.e)
VMEM_LIMIT = 56 * 1024 * 1024

TM_PROJ = 512
TQ_ATTN = 512
TC_CONV = 512
TM_OUT = 512
TM_FFN = 512
FF_CHUNK = 256
ACT_SLOTS = 3
HALO = 8
HALO_BF16 = 16


def _dot(a, b):
    return jnp.dot(a, b, preferred_element_type=F32)


def _dot_nt(a, b):
    return lax.dot_general(a, b, (((1,), (1,)), ((), ())), preferred_element_type=F32)


def _dot_tn(a, b):
    return lax.dot_general(a, b, (((0,), (0,)), ((), ())), preferred_element_type=F32)


def _split3(a):
    hi = a.astype(BF16)
    r = a - hi.astype(F32)
    mid = r.astype(BF16)
    lo = (r - mid.astype(F32)).astype(BF16)
    return hi, mid, lo


def _silu(x):
    return x / (1.0 + jnp.exp(-x))


def _softplus(x):
    return jnp.maximum(x, 0.0) + jnp.log1p(jnp.exp(-jnp.abs(x)))


def _params(sem):
    return pltpu.CompilerParams(dimension_semantics=sem, vmem_limit_bytes=VMEM_LIMIT)


def _const_spec(shape):
    nd = len(shape)
    return pl.BlockSpec(shape, lambda *_: (0,) * nd)


def _in_proj_kernel(x_ref, n1_ref, wqk_ref, wv_ref, wz_ref, wx_ref, wdt_ref, wdtt_ref, wmq_ref,
                    hsum_ref, hexp_ref, qkg_ref, mqg_ref, cos_ref, sin_ref,
                    q_ref, k_ref, v_ref, z_ref, xbc_ref, dt_ref, dtt_ref, mq_ref):
    x = x_ref[...]
    ms = jnp.mean(x * x, axis=-1, keepdims=True)
    h = (x * lax.rsqrt(ms + EPS) * n1_ref[...]).astype(BF16)

    qk = _dot(h, wqk_ref[...])
    ssq = _dot((qk * qk).astype(BF16), hsum_ref[...])
    inv = lax.rsqrt(ssq * (1.0 / HEAD_DIM) + EPS)
    inv_hi = inv.astype(BF16)
    inv_lo = (inv - inv_hi.astype(F32)).astype(BF16)
    scale = _dot(jnp.concatenate([inv_hi, inv_lo], axis=1), hexp_ref[...])
    qkn = qk * scale * qkg_ref[...]

    cos = cos_ref[...]
    sin = sin_ref[...]
    for p in range(GQA_GROUP):
        u = qkn[:, p * 256:p * 256 + LANES]
        w = qkn[:, p * 256 + LANES:(p + 1) * 256]
        q_ref[:, p * 256:p * 256 + LANES] = (u * cos - w * sin).astype(BF16)
        q_ref[:, p * 256 + LANES:(p + 1) * 256] = (w * cos + u * sin).astype(BF16)
    u = qkn[:, ATTN_WIDTH:ATTN_WIDTH + LANES]
    w = qkn[:, ATTN_WIDTH + LANES:QK_WIDTH]
    k_ref[:, :LANES] = (u * cos - w * sin).astype(BF16)
    k_ref[:, LANES:] = (w * cos + u * sin).astype(BF16)

    v_ref[...] = _dot(h, wv_ref[...]).astype(BF16)
    z_ref[...] = _dot(h, wz_ref[...]).astype(BF16)
    xbc_ref[...] = _dot(h, wx_ref[...]).astype(BF16)
    dt_ref[...] = _dot(h, wdt_ref[...])[:, :DT_WIDTH]
    dtt_ref[...] = _dot_nt(wdtt_ref[...], h)

    mq = _dot(h, wmq_ref[...])
    mqg = mqg_ref[...] * (MEM_HEAD_DIM ** -0.5 * LOG2E)
    for hh in range(MEM_HEADS):
        m = mq[:, hh * LANES:(hh + 1) * LANES]
        r = lax.rsqrt(jnp.mean(m * m, axis=-1, keepdims=True) + EPS)
        mq_ref[:, hh * LANES:(hh + 1) * LANES] = (m * r * mqg).astype(BF16)


def _in_proj(x2d, s, w):
    t = x2d.shape[0]
    tm = TM_PROJ
    nblk_seq = s // tm
    row = lambda i: (i, 0)
    tab = lambda i: (i % nblk_seq, 0)
    in_specs = [
        pl.BlockSpec((tm, D_MODEL), row),
        _const_spec((1, D_MODEL)),
        _const_spec((D_MODEL, QK_WIDTH)),
        _const_spec((D_MODEL, KV_WIDTH)),
        _const_spec((D_MODEL, SSM_INNER)),
        _const_spec((D_MODEL, XBC_WIDTH)),
        _const_spec((D_MODEL, LANES)),
        _const_spec((DT_WIDTH, D_MODEL)),
        _const_spec((D_MODEL, MEM_WIDTH)),
        _const_spec((QK_WIDTH, LANES)),
        _const_spec((2 * LANES, QK_WIDTH)),
        _const_spec((1, QK_WIDTH)),
        _const_spec((1, LANES)),
        pl.BlockSpec((tm, LANES), tab),
        pl.BlockSpec((tm, LANES), tab),
    ]
    out_shape = [
        jax.ShapeDtypeStruct((t, ATTN_WIDTH), BF16),
        jax.ShapeDtypeStruct((t, KV_WIDTH), BF16),
        jax.ShapeDtypeStruct((t, KV_WIDTH), BF16),
        jax.ShapeDtypeStruct((t, SSM_INNER), BF16),
        jax.ShapeDtypeStruct((t, XBC_WIDTH), BF16),
        jax.ShapeDtypeStruct((t, DT_WIDTH), F32),
        jax.ShapeDtypeStruct((DT_WIDTH, t), F32),
        jax.ShapeDtypeStruct((t, MEM_WIDTH), BF16),
    ]
    out_specs = [
        pl.BlockSpec((tm, ATTN_WIDTH), row),
        pl.BlockSpec((tm, KV_WIDTH), row),
        pl.BlockSpec((tm, KV_WIDTH), row),
        pl.BlockSpec((tm, SSM_INNER), row),
        pl.BlockSpec((tm, XBC_WIDTH), row),
        pl.BlockSpec((tm, DT_WIDTH), row),
        pl.BlockSpec((DT_WIDTH, tm), lambda i: (0, i)),
        pl.BlockSpec((tm, MEM_WIDTH), row),
    ]
    return pl.pallas_call(
        _in_proj_kernel, grid=(t // tm,), in_specs=in_specs, out_specs=out_specs, out_shape=out_shape,
        compiler_params=_params(("parallel",)), name="in_proj",
    )(x2d, w["n1"], w["wqk"], w["wv"], w["wz"], w["wx"], w["wdt"], w["wdtt"], w["wmq"],
      w["hsum"], w["hexp"], w["qkg"], w["mqg"], w["cos"][:s], w["sin"][:s])


def _mem_kv_kernel(mem_ref, nw_ref, w_ref, mkg_ref, mk_ref, mv_ref):
    x = mem_ref[0]
    ms = jnp.mean(x * x, axis=-1, keepdims=True)
    h = (x * lax.rsqrt(ms + EPS) * nw_ref[...]).astype(BF16)
    kv = _dot(h, w_ref[...])
    g = mkg_ref[...]
    for hh in range(MEM_HEADS):
        m = kv[:, hh * LANES:(hh + 1) * LANES]
        r = lax.rsqrt(jnp.mean(m * m, axis=-1, keepdims=True) + EPS)
        mk_ref[0, :, hh * LANES:(hh + 1) * LANES] = (m * r * g).astype(BF16)
    mv_ref[0] = kv[:, MEM_WIDTH:].astype(BF16)


def _mem_kv(mem, w):
    b = mem.shape[0]
    blk = lambda i: (i, 0, 0)
    return pl.pallas_call(
        _mem_kv_kernel, grid=(b,),
        in_specs=[pl.BlockSpec((1, N_MEM, D_MODEL), blk), _const_spec((1, D_MODEL)),
                  _const_spec((D_MODEL, 2 * MEM_WIDTH)), _const_spec((1, LANES))],
        out_specs=[pl.BlockSpec((1, N_MEM, MEM_WIDTH), blk), pl.BlockSpec((1, N_MEM, MEM_WIDTH), blk)],
        out_shape=[jax.ShapeDtypeStruct((b, N_MEM, MEM_WIDTH), BF16)] * 2,
        compiler_params=_params(("parallel",)), name="mem_kv",
    )(mem, w["memn"], w["wmemkv"], w["mkg"])


def _attention_kernel(sink_ref, q_ref, k_ref, v_ref, mq_ref, mk_ref, mv_ref, o_ref, mo_ref, *, seq):
    qi = pl.program_id(1)
    tq = q_ref.shape[1]
    kwin = 3 * BLOCK

    lane_k = lax.broadcasted_iota(jnp.int32, (kwin, 2 * LANES), 1)
    lane_o = lax.broadcasted_iota(jnp.int32, (BLOCK, 2 * LANES), 1)
    row_i = lax.broadcasted_iota(jnp.int32, (BLOCK, kwin), 0)
    col_i = lax.broadcasted_iota(jnp.int32, (BLOCK, kwin), 1)

    def block_body(blk, carry):
        r0 = qi * tq + blk * BLOCK
        ks = pl.multiple_of(jnp.clip(r0 - BLOCK, 0, seq - kwin), BLOCK)
        rq = pl.multiple_of(blk * BLOCK, BLOCK)
        kw = k_ref[0, pl.ds(ks, kwin), :]
        vw = v_ref[0, pl.ds(ks, kwin), :]
        delta = (col_i - row_i) + (ks - r0)
        bias = jnp.where(jnp.abs(delta) <= WINDOW, 0.0, NEG_BIG).astype(F32)
        zero = jnp.zeros_like(kw)
        kj = [jnp.where((lane_k % LANES) // HALF == j, kw, zero) for j in range(N_KV_HEADS)]
        v4 = jnp.concatenate([jnp.where(lane_k // HEAD_DIM == j, vw, zero) for j in range(N_KV_HEADS)], axis=0)
        for p in range(GQA_GROUP):
            qp = q_ref[0, pl.ds(rq, BLOCK), p * 256:(p + 1) * 256]
            probs = []
            invs = []
            for j in range(N_KV_HEADS):
                snk = sink_ref[j * GQA_GROUP + p] * LOG2E
                sc = _dot_nt(qp, kj[j]) + bias
                m = jnp.maximum(jnp.max(sc, axis=-1, keepdims=True), snk)
                e = jnp.exp2(sc - m)
                den = jnp.sum(e, axis=-1, keepdims=True) + jnp.exp2(snk - m)
                probs.append(e.astype(BF16))
                invs.append(1.0 / den)
            o = _dot(jnp.concatenate(probs, axis=1), v4)
            inv = jnp.where(lane_o < HEAD_DIM, invs[0],
                            jnp.where(lane_o < 2 * HEAD_DIM, invs[1],
                                      jnp.where(lane_o < 3 * HEAD_DIM, invs[2], invs[3])))
            o_ref[0, pl.ds(rq, BLOCK), p * 256:(p + 1) * 256] = (o * inv).astype(BF16)
        return carry

    lax.fori_loop(0, tq // BLOCK, block_body, 0)

    for hh in range(MEM_HEADS):
        qm = mq_ref[0, :, hh * LANES:(hh + 1) * LANES]
        km = mk_ref[0, :, hh * LANES:(hh + 1) * LANES]
        vm = mv_ref[0, :, hh * LANES:(hh + 1) * LANES]
        sc = _dot_nt(qm, km)
        m = jnp.max(sc, axis=-1, keepdims=True)
        e = jnp.exp2(sc - m)
        den = jnp.sum(e, axis=-1, keepdims=True)
        o = _dot(e.astype(BF16), vm)
        mo_ref[0, :, hh * LANES:(hh + 1) * LANES] = (o * (1.0 / den)).astype(BF16)


def _attention(q, k, v, mq, mk, mv, sink):
    b, s, _ = q.shape
    tq = TQ_ATTN
    qblk = lambda bi, i: (bi, i, 0)
    full = lambda bi, i: (bi, 0, 0)
    return pl.pallas_call(
        functools.partial(_attention_kernel, seq=s), grid=(b, s // tq),
        in_specs=[pl.BlockSpec(memory_space=pltpu.SMEM),
                  pl.BlockSpec((1, tq, ATTN_WIDTH), qblk),
                  pl.BlockSpec((1, s, KV_WIDTH), full),
                  pl.BlockSpec((1, s, KV_WIDTH), full),
                  pl.BlockSpec((1, tq, MEM_WIDTH), qblk),
                  pl.BlockSpec((1, N_MEM, MEM_WIDTH), full),
                  pl.BlockSpec((1, N_MEM, MEM_WIDTH), full)],
        out_specs=[pl.BlockSpec((1, tq, ATTN_WIDTH), qblk), pl.BlockSpec((1, tq, MEM_WIDTH), qblk)],
        out_shape=[jax.ShapeDtypeStruct((b, s, ATTN_WIDTH), BF16), jax.ShapeDtypeStruct((b, s, MEM_WIDTH), BF16)],
        compiler_params=_params(("parallel", "arbitrary")), name="attention",
    )(sink, q, k, v, mq, mk, mv)


def _ssd_conv_kernel(x_ref, prev_ref, next_ref, w_ref, b_ref, o_ref, buf_ref):
    ci = pl.program_id(1)
    nci = pl.num_programs(1)
    tc = x_ref.shape[1]
    hb = HALO_BF16
    pad = SSM_CONV // 2
    buf_ref[0:hb, :] = prev_ref[0].astype(F32) * (ci > 0).astype(F32)
    buf_ref[hb:hb + tc, :] = x_ref[0].astype(F32)
    buf_ref[hb + tc:, :] = next_ref[0].astype(F32) * (ci < nci - 1).astype(F32)
    y = b_ref[...] + buf_ref[pl.ds(hb - pad, tc), :] * w_ref[0:1, :]
    for t in range(1, SSM_CONV):
        y = y + buf_ref[pl.ds(hb - pad + t, tc), :] * w_ref[t:t + 1, :]
    o_ref[0] = _silu(y).astype(BF16)


def _ssd_conv(xbc, w):
    b, s, _ = xbc.shape
    tc = TC_CONV
    r = tc // HALO_BF16
    nh = s // HALO_BF16
    return pl.pallas_call(
        _ssd_conv_kernel, grid=(b, s // tc),
        in_specs=[pl.BlockSpec((1, tc, XBC_WIDTH), lambda bi, i: (bi, i, 0)),
                  pl.BlockSpec((1, HALO_BF16, XBC_WIDTH), lambda bi, i: (bi, jnp.maximum(i * r - 1, 0), 0)),
                  pl.BlockSpec((1, HALO_BF16, XBC_WIDTH), lambda bi, i: (bi, jnp.minimum((i + 1) * r, nh - 1), 0)),
                  _const_spec((SSM_CONV, XBC_WIDTH)), _const_spec((1, XBC_WIDTH))],
        out_specs=pl.BlockSpec((1, tc, XBC_WIDTH), lambda bi, i: (bi, i, 0)),
        out_shape=jax.ShapeDtypeStruct((b, s, XBC_WIDTH), BF16),
        scratch_shapes=[pltpu.VMEM((tc + 2 * HALO_BF16, XBC_WIDTH), F32)],
        compiler_params=_params(("parallel", "parallel")), name="ssd_conv",
    )(xbc, xbc, xbc, w["convw"], w["convb"])


def _slab(cols, first):
    lane = lax.broadcasted_iota(jnp.int32, (SSM_CHUNK, GROUP_INNER), 1)
    c = [cols[:, first + e:first + e + 1] for e in range(HEADS_PER_STATE_GROUP)]
    return jnp.where(lane < 64, c[0], jnp.where(lane < 128, c[1], jnp.where(lane < 192, c[2], c[3])))


def _slab_row(row, first):
    lane = lax.broadcasted_iota(jnp.int32, (1, GROUP_INNER), 1)
    c = [row[:, first + e:first + e + 1] for e in range(HEADS_PER_STATE_GROUP)]
    return jnp.where(lane < 64, c[0], jnp.where(lane < 128, c[1], jnp.where(lane < 192, c[2], c[3])))


def _ssd_scan_kernel(xf_ref, xb_ref, dtf_ref, dtb_ref, dttf_ref, bias_r_ref, bias_c_ref,
                     alog_r_ref, alog_c_ref, dskip_ref, ya_ref, yb_ref, hf_ref, hb_ref):
    c = pl.program_id(1)

    @pl.when(c == 0)
    def _():
        hf_ref[...] = jnp.zeros_like(hf_ref)
        hb_ref[...] = jnp.zeros_like(hb_ref)

    ln = SSM_CHUNK
    ri = lax.broadcasted_iota(jnp.int32, (ln, ln), 0)
    cj = lax.broadcasted_iota(jnp.int32, (ln, ln), 1)
    low_incl = (cj <= ri)
    tri_l = jnp.where(low_incl, 1.0, 0.0).astype(BF16)
    tri_u = jnp.where(cj >= ri, 1.0, 0.0).astype(BF16)

    a_row = -jnp.exp(alog_r_ref[...])
    a_col = -jnp.exp(alog_c_ref[...])

    def exact_left(m, a):
        hi, mid, lo = _split3(a)
        return _dot(m, hi) + _dot(m, mid) + _dot(m, lo)

    def exact_right(a, m):
        hi, mid, lo = _split3(a)
        return _dot(hi, m) + _dot(mid, m) + _dot(lo, m)

    xc = xf_ref[0]
    xs = xc[:, :SSM_INNER]
    dt_c = _softplus(dtf_ref[0] + bias_r_ref[...])
    dt_r = _softplus(dttf_ref[...] + bias_c_ref[...])
    a_c = dt_c * a_row
    a_r = dt_r * a_col
    cs_c = exact_left(tri_l, a_c)
    sf_c = exact_left(tri_u, a_c)
    cs_r = exact_right(a_r, tri_u)
    sf_r = exact_right(a_r, tri_l)

    lt = cj < ri
    gt = cj > ri
    y_parts = []
    for g in range(SSM_GROUPS):
        bm = xc[:, SSM_INNER + g * SSM_STATE:SSM_INNER + (g + 1) * SSM_STATE]
        cm = xc[:, SSM_INNER + SSM_GROUPS * SSM_STATE + g * SSM_STATE:SSM_INNER + SSM_GROUPS * SSM_STATE + (g + 1) * SSM_STATE]
        cb = _dot_nt(cm, bm)
        for pair in range(HEADS_PER_STATE_GROUP // 2):
            ms = []
            for e in range(2):
                hd = g * HEADS_PER_STATE_GROUP + pair * 2 + e
                hb_i = SSM_HEADS + hd
                dl = cs_c[:, hd:hd + 1] - cs_r[hd:hd + 1, :]
                du = sf_c[:, hb_i:hb_i + 1] - sf_r[hb_i:hb_i + 1, :]
                ex = jnp.exp(jnp.where(low_incl, dl, du))
                dtf_j = dt_r[hd:hd + 1, :]
                dtb_j = dt_r[hb_i:hb_i + 1, :]
                dts = jnp.where(lt, dtf_j, jnp.where(gt, dtb_j, dtf_j + dtb_j))
                ms.append((cb * ex * dts).astype(BF16))
            col0 = (g * HEADS_PER_STATE_GROUP + pair * 2) * SSM_HEAD_DIM
            xp = xs[:, col0:col0 + LANES]
            lane = lax.broadcasted_iota(jnp.int32, (ln, LANES), 1)
            zero = jnp.zeros_like(xp)
            xbd = jnp.concatenate([jnp.where(lane < SSM_HEAD_DIM, xp, zero),
                                   jnp.where(lane >= SSM_HEAD_DIM, xp, zero)], axis=0)
            y_parts.append(_dot(jnp.concatenate(ms, axis=1), xbd))
    y = jnp.concatenate(y_parts, axis=1)
    xs32 = xs.astype(F32)
    y = y + xs32 * dskip_ref[...]

    e_cs = jnp.exp(cs_c)
    last = cs_c[ln - 1:ln, :]
    w_state = dt_c * jnp.exp(last - cs_c)
    e_last = jnp.exp(last)
    y_off = []
    for g in range(SSM_GROUPS):
        first = g * HEADS_PER_STATE_GROUP
        bm = xc[:, SSM_INNER + g * SSM_STATE:SSM_INNER + (g + 1) * SSM_STATE]
        cm = xc[:, SSM_INNER + SSM_GROUPS * SSM_STATE + g * SSM_STATE:SSM_INNER + SSM_GROUPS * SSM_STATE + (g + 1) * SSM_STATE]
        hprev = hf_ref[g]
        y_off.append(_dot(cm, hprev.astype(BF16)) * _slab(e_cs, first))
        xw = (xs32[:, g * GROUP_INNER:(g + 1) * GROUP_INNER] * _slab(w_state, first)).astype(BF16)
        hf_ref[g] = hprev * _slab_row(e_last, first) + _dot_tn(bm, xw)
    ya_ref[0] = y + jnp.concatenate(y_off, axis=1)

    xc = xb_ref[0]
    xs32 = xc[:, :SSM_INNER].astype(F32)
    dt_c = _softplus(dtb_ref[0] + bias_r_ref[...])
    a_c = dt_c * a_row
    sf_c = exact_left(tri_u, a_c)
    e_sf = jnp.exp(sf_c)
    head0 = sf_c[0:1, :]
    w_state = dt_c * jnp.exp(head0 - sf_c)
    e_head = jnp.exp(head0)
    y_off = []
    for g in range(SSM_GROUPS):
        first = SSM_HEADS + g * HEADS_PER_STATE_GROUP
        bm = xc[:, SSM_INNER + g * SSM_STATE:SSM_INNER + (g + 1) * SSM_STATE]
        cm = xc[:, SSM_INNER + SSM_GROUPS * SSM_STATE + g * SSM_STATE:SSM_INNER + SSM_GROUPS * SSM_STATE + (g + 1) * SSM_STATE]
        hprev = hb_ref[g]
        y_off.append(_dot(cm, hprev.astype(BF16)) * _slab(e_sf, first))
        xw = (xs32[:, g * GROUP_INNER:(g + 1) * GROUP_INNER] * _slab(w_state, first)).astype(BF16)
        hb_ref[g] = hprev * _slab_row(e_head, first) + _dot_tn(bm, xw)
    yb_ref[0] = jnp.concatenate(y_off, axis=1)


def _ssd_scan(xc, dt, dtt, w):
    b, s, _ = xc.shape
    nc = s // SSM_CHUNK
    fwd = lambda bi, c: (bi, c, 0)
    bwd = lambda bi, c: (bi, nc - 1 - c, 0)
    return pl.pallas_call(
        _ssd_scan_kernel, grid=(b, nc),
        in_specs=[pl.BlockSpec((1, SSM_CHUNK, XBC_WIDTH), fwd),
                  pl.BlockSpec((1, SSM_CHUNK, XBC_WIDTH), bwd),
                  pl.BlockSpec((1, SSM_CHUNK, DT_WIDTH), fwd),
                  pl.BlockSpec((1, SSM_CHUNK, DT_WIDTH), bwd),
                  pl.BlockSpec((DT_WIDTH, SSM_CHUNK), lambda bi, c: (0, bi * nc + c)),
                  _const_spec((1, DT_WIDTH)), _const_spec((DT_WIDTH, 1)),
                  _const_spec((1, DT_WIDTH)), _const_spec((DT_WIDTH, 1)),
                  _const_spec((1, SSM_INNER))],
        out_specs=[pl.BlockSpec((1, SSM_CHUNK, SSM_INNER), fwd), pl.BlockSpec((1, SSM_CHUNK, SSM_INNER), bwd)],
        out_shape=[jax.ShapeDtypeStruct((b, s, SSM_INNER), F32)] * 2,
        scratch_shapes=[pltpu.VMEM((SSM_GROUPS, SSM_STATE, GROUP_INNER), F32)] * 2,
        compiler_params=_params(("parallel", "arbitrary")), name="ssd_scan",
    )(xc, xc, dt, dt, dtt, w["dtb_r"], w["dtb_c"], w["alog_r"], w["alog_c"], w["dskip"])


def _out_proj_kernel(x_ref, attn_ref, ya_ref, yb_ref, z_ref, mem_ref, sn_ref, wa_ref, ws_ref, wm_ref, o_ref):
    y = (ya_ref[...] + yb_ref[...]) * _silu(z_ref[...].astype(F32))
    ms = jnp.mean(y * y, axis=-1, keepdims=True)
    ssm = (y * lax.rsqrt(ms + EPS) * sn_ref[...]).astype(BF16)
    acc = _dot(attn_ref[...], wa_ref[...]) + _dot(ssm, ws_ref[...]) + _dot(mem_ref[...], wm_ref[...])
    o_ref[...] = x_ref[...] + acc


def _out_proj(x2d, attn, ya, yb, z, memo, w):
    t = x2d.shape[0]
    tm = TM_OUT
    row = lambda i: (i, 0)
    return pl.pallas_call(
        _out_proj_kernel, grid=(t // tm,),
        in_specs=[pl.BlockSpec((tm, D_MODEL), row), pl.BlockSpec((tm, ATTN_WIDTH), row),
                  pl.BlockSpec((tm, SSM_INNER), row), pl.BlockSpec((tm, SSM_INNER), row),
                  pl.BlockSpec((tm, SSM_INNER), row), pl.BlockSpec((tm, MEM_WIDTH), row),
                  _const_spec((1, SSM_INNER)), _const_spec((ATTN_WIDTH, D_MODEL)),
                  _const_spec((SSM_INNER, D_MODEL)), _const_spec((MEM_WIDTH, D_MODEL))],
        out_specs=pl.BlockSpec((tm, D_MODEL), row),
        out_shape=jax.ShapeDtypeStruct((t, D_MODEL), F32),
        compiler_params=_params(("parallel",)), name="out_proj",
    )(x2d, attn, ya, yb, z, memo, w["ssmn"], w["wo_a"], w["wo_s"], w["wo_m"])


def _ffn_kernel(x_ref, prev_ref, next_ref, n2_ref, wg_ref, wu_ref, wd_ref, cw_ref, cb_ref, o_ref,
                g_ref, u_ref, act_ref, acc_ref):
    i = pl.program_id(1)
    ni = pl.num_programs(1)
    tm = x_ref.shape[1]
    n2 = n2_ref[...]

    def norm(v):
        ms = jnp.mean(v * v, axis=-1, keepdims=True)
        return (v * lax.rsqrt(ms + EPS) * n2).astype(BF16)

    keep_prev = (i > 0).astype(F32)
    keep_next = (i < ni - 1).astype(F32)
    hl = HALO_BF16
    pad = FFN_CONV // 2

    h = jnp.concatenate([norm(prev_ref[0]), norm(x_ref[0]), norm(next_ref[0])], axis=0)
    n_chunks = D_FF // FF_CHUNK

    def up(cix):
        c0 = cix * FF_CHUNK
        slot = cix % 2
        for dst, wref in ((g_ref, wg_ref), (u_ref, wu_ref)):
            r = _dot(h, wref[:, c0:c0 + FF_CHUNK])
            dst[slot, 0:hl, :] = r[0:hl, :] * keep_prev
            dst[slot, hl:hl + tm, :] = r[hl:hl + tm, :]
            dst[slot, hl + tm:, :] = r[hl + tm:, :] * keep_next

    def conv_act(cix):
        c0 = cix * FF_CHUNK
        slot = cix % 2
        outs = []
        for src, off in ((g_ref, 0), (u_ref, D_FF)):
            cw = cw_ref[:, off + c0:off + c0 + FF_CHUNK]
            cb = cb_ref[:, off + c0:off + c0 + FF_CHUNK]
            y = cb + src[slot, pl.ds(hl - pad, tm), :] * cw[0:1, :]
            for t in range(1, FFN_CONV):
                y = y + src[slot, pl.ds(hl - pad + t, tm), :] * cw[t:t + 1, :]
            outs.append(y)
        act_ref[cix % ACT_SLOTS] = (_silu(outs[0]) * outs[1]).astype(BF16)

    def down(cix):
        c0 = cix * FF_CHUNK
        part = _dot(act_ref[cix % ACT_SLOTS], wd_ref[c0:c0 + FF_CHUNK, :])
        if cix == 0:
            acc_ref[...] = part
        else:
            acc_ref[...] += part

    lag = ACT_SLOTS - 1
    up(0)
    for cix in range(n_chunks):
        if cix + 1 < n_chunks:
            up(cix + 1)
        if cix >= lag:
            down(cix - lag)
        conv_act(cix)
    for cix in range(n_chunks - lag, n_chunks):
        down(cix)
    o_ref[0] = x_ref[0] + acc_ref[...]


def _ffn(x1, w):
    b, s, _ = x1.shape
    tm = TM_FFN
    hl = HALO_BF16
    r = tm // hl
    nh = s // hl
    return pl.pallas_call(
        _ffn_kernel, grid=(b, s // tm),
        in_specs=[pl.BlockSpec((1, tm, D_MODEL), lambda bi, i: (bi, i, 0)),
                  pl.BlockSpec((1, hl, D_MODEL), lambda bi, i: (bi, jnp.maximum(i * r - 1, 0), 0)),
                  pl.BlockSpec((1, hl, D_MODEL), lambda bi, i: (bi, jnp.minimum((i + 1) * r, nh - 1), 0)),
                  _const_spec((1, D_MODEL)),
                  _const_spec((D_MODEL, D_FF)), _const_spec((D_MODEL, D_FF)), _const_spec((D_FF, D_MODEL)),
                  _const_spec((FFN_CONV, 2 * D_FF)), _const_spec((1, 2 * D_FF))],
        out_specs=pl.BlockSpec((1, tm, D_MODEL), lambda bi, i: (bi, i, 0)),
        out_shape=jax.ShapeDtypeStruct((b, s, D_MODEL), F32),
        scratch_shapes=[pltpu.VMEM((2, tm + 2 * hl, FF_CHUNK), F32),
                        pltpu.VMEM((2, tm + 2 * hl, FF_CHUNK), F32),
                        pltpu.VMEM((ACT_SLOTS, tm, FF_CHUNK), BF16),
                        pltpu.VMEM((tm, D_MODEL), F32)],
        compiler_params=_params(("parallel", "parallel")), name="ffn",
    )(x1, x1, x1, w["n2"], w["wg"], w["wu"], w["wd"], w["fcw"], w["fcb"])


def _prep_weights(max_seq, norm1_w, w_in, q_norm_w, k_norm_w, attn_sink, ssm_conv_w, ssm_conv_b, ssm_dt_bias,
                  ssm_A_log, ssm_D, ssm_norm_w, mem_norm_w, w_mem_kv, mq_norm_w, mk_norm_w, w_out, norm2_w,
                  w_ffn_up, ffn_conv_w, ffn_conv_b, w_ffn_down):
    o = 0
    wq = w_in[:, o:o + ATTN_WIDTH]; o += ATTN_WIDTH
    wk = w_in[:, o:o + KV_WIDTH]; o += KV_WIDTH
    wv = w_in[:, o:o + KV_WIDTH]; o += KV_WIDTH
    wz = w_in[:, o:o + SSM_INNER]; o += SSM_INNER
    wx = w_in[:, o:o + XBC_WIDTH]; o += XBC_WIDTH
    wdt = w_in[:, o:o + DT_WIDTH]; o += DT_WIDTH
    wmq = w_in[:, o:o + MEM_WIDTH]

    wq_p = wq.reshape(D_MODEL, N_KV_HEADS, GQA_GROUP, 2, HALF).transpose(0, 2, 3, 1, 4).reshape(D_MODEL, ATTN_WIDTH)
    wk_p = wk.reshape(D_MODEL, N_KV_HEADS, 2, HALF).transpose(0, 2, 1, 3).reshape(D_MODEL, KV_WIDTH)
    qg = jnp.broadcast_to(q_norm_w.reshape(1, 2, 1, HALF), (GQA_GROUP, 2, N_KV_HEADS, HALF)).reshape(ATTN_WIDTH)
    kg = jnp.broadcast_to(k_norm_w.reshape(2, 1, HALF), (2, N_KV_HEADS, HALF)).reshape(KV_WIDTH)
    qkg = jnp.concatenate([qg * (HEAD_DIM ** -0.5 * LOG2E), kg]).reshape(1, QK_WIDTH)

    cq = np.arange(ATTN_WIDTH)
    eq = (cq // 256) * N_KV_HEADS + (cq % LANES) // HALF
    ck = np.arange(KV_WIDTH)
    ek = N_Q_HEADS + (ck % LANES) // HALF
    e_all = np.concatenate([eq, ek])
    hsum = np.zeros((QK_WIDTH, LANES), np.float32)
    hsum[np.arange(QK_WIDTH), e_all] = 1.0
    hexp = np.concatenate([hsum.T, hsum.T], axis=0)

    inv = ROPE_THETA ** (-jnp.arange(0, HEAD_DIM, 2, dtype=F32) / HEAD_DIM)
    ang = jnp.arange(max_seq, dtype=F32)[:, None] * inv[None, :]
    cos = jnp.tile(jnp.cos(ang), (1, LANES // HALF))
    sin = jnp.tile(jnp.sin(ang), (1, LANES // HALF))

    wo_a = w_out[:ATTN_WIDTH].reshape(N_KV_HEADS, GQA_GROUP, HEAD_DIM, D_MODEL).transpose(1, 0, 2, 3)
    wo_a = wo_a.reshape(ATTN_WIDTH, D_MODEL)

    wdt_pad = jnp.zeros((D_MODEL, LANES), F32).at[:, :DT_WIDTH].set(wdt)
    return {
        "n1": norm1_w.reshape(1, D_MODEL),
        "wqk": jnp.concatenate([wq_p, wk_p], axis=1).astype(BF16),
        "wv": wv.astype(BF16), "wz": wz.astype(BF16), "wx": wx.astype(BF16),
        "wdt": wdt_pad.astype(BF16), "wdtt": wdt.T.astype(BF16), "wmq": wmq.astype(BF16),
        "hsum": jnp.asarray(hsum, BF16), "hexp": jnp.asarray(hexp, BF16),
        "qkg": qkg, "mqg": mq_norm_w.reshape(1, LANES), "cos": cos, "sin": sin,
        "sink": attn_sink,
        "memn": mem_norm_w.reshape(1, D_MODEL), "wmemkv": w_mem_kv.astype(BF16), "mkg": mk_norm_w.reshape(1, LANES),
        "convw": ssm_conv_w, "convb": ssm_conv_b.reshape(1, XBC_WIDTH),
        "dtb_r": ssm_dt_bias.reshape(1, DT_WIDTH), "dtb_c": ssm_dt_bias.reshape(DT_WIDTH, 1),
        "alog_r": ssm_A_log.reshape(1, DT_WIDTH), "alog_c": ssm_A_log.reshape(DT_WIDTH, 1),
        "dskip": jnp.repeat(ssm_D, SSM_HEAD_DIM).reshape(1, SSM_INNER),
        "ssmn": ssm_norm_w.reshape(1, SSM_INNER),
        "wo_a": wo_a.astype(BF16),
        "wo_s": w_out[ATTN_WIDTH:ATTN_WIDTH + SSM_INNER].astype(BF16),
        "wo_m": w_out[ATTN_WIDTH + SSM_INNER:].astype(BF16),
        "n2": norm2_w.reshape(1, D_MODEL),
        "wg": w_ffn_up[:, :D_FF].astype(BF16), "wu": w_ffn_up[:, D_FF:].astype(BF16),
        "wd": w_ffn_down.astype(BF16),
        "fcw": ffn_conv_w, "fcb": ffn_conv_b.reshape(1, 2 * D_FF),
    }


def _encoder_layer(x, mem, w):
    b, s, _ = x.shape
    t = b * s
    x2d = x.reshape(t, D_MODEL)
    q, k, v, z, xbc, dt, dtt, mq = _in_proj(x2d, s, w)
    mk, mv = _mem_kv(mem, w)
    attn, memo = _attention(q.reshape(b, s, -1), k.reshape(b, s, -1), v.reshape(b, s, -1),
                            mq.reshape(b, s, -1), mk, mv, w["sink"])
    xc = _ssd_conv(xbc.reshape(b, s, -1), w)
    ya, yb = _ssd_scan(xc, dt.reshape(b, s, -1), dtt, w)
    x1 = _out_proj(x2d, attn.reshape(t, -1), ya.reshape(t, -1), yb.reshape(t, -1), z, memo.reshape(t, -1), w)
    return _ffn(x1.reshape(b, s, D_MODEL), w)


def kernel(x_prompt, x_sample, mem_prompt, mem_sample, norm1_w, w_in, q_norm_w, k_norm_w, attn_sink, ssm_conv_w, ssm_conv_b, ssm_dt_bias, ssm_A_log, ssm_D, ssm_norm_w, mem_norm_w, w_mem_kv, mq_norm_w, mk_norm_w, w_out, norm2_w, w_ffn_up, ffn_conv_w, ffn_conv_b, w_ffn_down):
    weights = (norm1_w, w_in, q_norm_w, k_norm_w, attn_sink, ssm_conv_w, ssm_conv_b, ssm_dt_bias,
               ssm_A_log, ssm_D, ssm_norm_w, mem_norm_w, w_mem_kv, mq_norm_w, mk_norm_w, w_out,
               norm2_w, w_ffn_up, ffn_conv_w, ffn_conv_b, w_ffn_down)
    depth = norm1_w.shape[0]
    max_seq = max(x_prompt.shape[1], x_sample.shape[1])
    y_prompt, y_sample = x_prompt, x_sample
    for layer in range(depth):
        w = _prep_weights(max_seq, *[p[layer] for p in weights])
        y_prompt = _encoder_layer(y_prompt, mem_prompt, w)
        y_sample = _encoder_layer(y_sample, mem_sample, w)
    return (y_prompt, y_sample)
```

```python
import functools
import math

import numpy as np
import jax
import jax.numpy as jnp
from jax import lax
from jax.experimental import pallas as pl
from jax.experimental.pallas import tpu as pltpu

F32 = jnp.float32
BF16 = jnp.bfloat16

D_MODEL = 1024
HEAD_DIM = 64
HALF = HEAD_DIM // 2
N_Q_HEADS = 16
N_KV_HEADS = 4
GQA_GROUP = N_Q_HEADS // N_KV_HEADS
ATTN_WIDTH = N_Q_HEADS * HEAD_DIM
KV_WIDTH = N_KV_HEADS * HEAD_DIM
WINDOW = 128
BLOCK = 128
ROPE_THETA = 10000.0
SSM_HEADS = 8
SSM_HEAD_DIM = 64
SSM_INNER = SSM_HEADS * SSM_HEAD_DIM
SSM_GROUPS = 2
SSM_STATE = 128
SSM_CONV = 5
SSM_CHUNK = 128
XBC_WIDTH = SSM_INNER + 2 * SSM_GROUPS * SSM_STATE
DT_WIDTH = 2 * SSM_HEADS
N_MEM = 256
MEM_HEADS = 4
MEM_HEAD_DIM = 128
MEM_WIDTH = MEM_HEADS * MEM_HEAD_DIM
D_FF = 2816
FFN_CONV = 3
EPS = 1e-6

LANES = 128
QK_WIDTH = ATTN_WIDTH + KV_WIDTH
HEADS_PER_STATE_GROUP = SSM_HEADS // SSM_GROUPS
GROUP_INNER = HEADS_PER_STATE_GROUP * SSM_HEAD_DIM
NEG_BIG = -1e30
LOG2E = math.log2(math.e)
VMEM_LIMIT = 56 * 1024 * 1024

TM_PROJ = 512
TQ_ATTN = 512
TC_CONV = 512
TS_SCAN = 1024
TM_OUT = 512
TM_FFN = 512
FF_CHUNK = 256
ACT_SLOTS = 3
HALO = 8
HALO_BF16 = 16


def _dot(a, b):
    return jnp.dot(a, b, preferred_element_type=F32)


def _dot_nt(a, b):
    return lax.dot_general(a, b, (((1,), (1,)), ((), ())), preferred_element_type=F32)


def _dot_tn(a, b):
    return lax.dot_general(a, b, (((0,), (0,)), ((), ())), preferred_element_type=F32)


def _split3(a):
    hi = a.astype(BF16)
    r = a - hi.astype(F32)
    mid = r.astype(BF16)
    lo = (r - mid.astype(F32)).astype(BF16)
    return hi, mid, lo


def _silu(x):
    return x / (1.0 + jnp.exp(-x))


def _softplus(x):
    return jnp.maximum(x, 0.0) + jnp.log1p(jnp.exp(-jnp.abs(x)))


def _shift_rows(xe, off, n, delta):
    cur = xe[off:off + n]
    if delta == 0:
        return cur
    r = lax.broadcasted_iota(jnp.int32, cur.shape, 0) % HALO
    if delta < 0:
        merged = jnp.where(r < HALO + delta, cur, xe[off - HALO:off + n - HALO])
        rot = -delta
    else:
        merged = jnp.where(r >= delta, cur, xe[off + HALO:off + n + HALO])
        rot = HALO - delta
    width = cur.shape[1]
    return pltpu.roll(merged.reshape(n // HALO, HALO, width), rot, axis=1).reshape(n, width)


def _params(sem):
    return pltpu.CompilerParams(dimension_semantics=sem, vmem_limit_bytes=VMEM_LIMIT)


def _const_spec(shape):
    nd = len(shape)
    return pl.BlockSpec(shape, lambda *_: (0,) * nd)


def _in_proj_kernel(x_ref, n1_ref, wqk_ref, wv_ref, wz_ref, wx_ref, wdtt_ref, wmq_ref,
                    hsum_ref, hexp_ref, qkg_ref, mqg_ref, cos_ref, sin_ref,
                    q_ref, k_ref, v_ref, z_ref, xbc_ref, dtt_ref, mq_ref):
    x = x_ref[...]
    ms = jnp.mean(x * x, axis=-1, keepdims=True)
    h = (x * lax.rsqrt(ms + EPS) * n1_ref[...]).astype(BF16)

    qk = _dot(h, wqk_ref[...])
    ssq = _dot((qk * qk).astype(BF16), hsum_ref[...])
    inv = lax.rsqrt(ssq * (1.0 / HEAD_DIM) + EPS)
    inv_hi = inv.astype(BF16)
    inv_lo = (inv - inv_hi.astype(F32)).astype(BF16)
    scale = _dot(jnp.concatenate([inv_hi, inv_lo], axis=1), hexp_ref[...])
    qkn = qk * scale * qkg_ref[...]

    cos = cos_ref[...]
    sin = sin_ref[...]
    for p in range(GQA_GROUP):
        u = qkn[:, p * 256:p * 256 + LANES]
        w = qkn[:, p * 256 + LANES:(p + 1) * 256]
        q_ref[:, p * 256:p * 256 + LANES] = (u * cos - w * sin).astype(BF16)
        q_ref[:, p * 256 + LANES:(p + 1) * 256] = (w * cos + u * sin).astype(BF16)
    u = qkn[:, ATTN_WIDTH:ATTN_WIDTH + LANES]
    w = qkn[:, ATTN_WIDTH + LANES:QK_WIDTH]
    k_ref[:, :LANES] = (u * cos - w * sin).astype(BF16)
    k_ref[:, LANES:] = (w * cos + u * sin).astype(BF16)

    v_ref[...] = _dot(h, wv_ref[...]).astype(BF16)
    z_ref[...] = _dot(h, wz_ref[...]).astype(BF16)
    xbc_ref[...] = _dot(h, wx_ref[...]).astype(BF16)
    dtt_ref[...] = _dot_nt(wdtt_ref[...], h)

    mq = _dot(h, wmq_ref[...])
    mqg = mqg_ref[...] * (MEM_HEAD_DIM ** -0.5 * LOG2E)
    for hh in range(MEM_HEADS):
        m = mq[:, hh * LANES:(hh + 1) * LANES]
        r = lax.rsqrt(jnp.mean(m * m, axis=-1, keepdims=True) + EPS)
        mq_ref[:, hh * LANES:(hh + 1) * LANES] = (m * r * mqg).astype(BF16)


def _in_proj(x2d, s, w):
    t = x2d.shape[0]
    tm = TM_PROJ
    nblk_seq = s // tm
    row = lambda i: (i, 0)
    tab = lambda i: (i % nblk_seq, 0)
    in_specs = [
        pl.BlockSpec((tm, D_MODEL), row),
        _const_spec((1, D_MODEL)),
        _const_spec((D_MODEL, QK_WIDTH)),
        _const_spec((D_MODEL, KV_WIDTH)),
        _const_spec((D_MODEL, SSM_INNER)),
        _const_spec((D_MODEL, XBC_WIDTH)),
        _const_spec((DT_WIDTH, D_MODEL)),
        _const_spec((D_MODEL, MEM_WIDTH)),
        _const_spec((QK_WIDTH, LANES)),
        _const_spec((2 * LANES, QK_WIDTH)),
        _const_spec((1, QK_WIDTH)),
        _const_spec((1, LANES)),
        pl.BlockSpec((tm, LANES), tab),
        pl.BlockSpec((tm, LANES), tab),
    ]
    out_shape = [
        jax.ShapeDtypeStruct((t, ATTN_WIDTH), BF16),
        jax.ShapeDtypeStruct((t, KV_WIDTH), BF16),
        jax.ShapeDtypeStruct((t, KV_WIDTH), BF16),
        jax.ShapeDtypeStruct((t, SSM_INNER), BF16),
        jax.ShapeDtypeStruct((t, XBC_WIDTH), BF16),
        jax.ShapeDtypeStruct((DT_WIDTH, t), F32),
        jax.ShapeDtypeStruct((t, MEM_WIDTH), BF16),
    ]
    out_specs = [
        pl.BlockSpec((tm, ATTN_WIDTH), row),
        pl.BlockSpec((tm, KV_WIDTH), row),
        pl.BlockSpec((tm, KV_WIDTH), row),
        pl.BlockSpec((tm, SSM_INNER), row),
        pl.BlockSpec((tm, XBC_WIDTH), row),
        pl.BlockSpec((DT_WIDTH, tm), lambda i: (0, i)),
        pl.BlockSpec((tm, MEM_WIDTH), row),
    ]
    return pl.pallas_call(
        _in_proj_kernel, grid=(t // tm,), in_specs=in_specs, out_specs=out_specs, out_shape=out_shape,
        compiler_params=_params(("parallel",)), name="in_proj",
    )(x2d, w["n1"], w["wqk"], w["wv"], w["wz"], w["wx"], w["wdtt"], w["wmq"],
      w["hsum"], w["hexp"], w["qkg"], w["mqg"], w["cos"][:s], w["sin"][:s])


def _mem_kv_kernel(mem_ref, nw_ref, w_ref, mkg_ref, mk_ref, mv_ref):
    x = mem_ref[0]
    ms = jnp.mean(x * x, axis=-1, keepdims=True)
    h = (x * lax.rsqrt(ms + EPS) * nw_ref[...]).astype(BF16)
    kv = _dot(h, w_ref[...])
    g = mkg_ref[...]
    for hh in range(MEM_HEADS):
        m = kv[:, hh * LANES:(hh + 1) * LANES]
        r = lax.rsqrt(jnp.mean(m * m, axis=-1, keepdims=True) + EPS)
        mk_ref[0, :, hh * LANES:(hh + 1) * LANES] = (m * r * g).astype(BF16)
    mv_ref[0] = kv[:, MEM_WIDTH:].astype(BF16)


def _mem_kv(mem, w):
    b = mem.shape[0]
    blk = lambda i: (i, 0, 0)
    return pl.pallas_call(
        _mem_kv_kernel, grid=(b,),
        in_specs=[pl.BlockSpec((1, N_MEM, D_MODEL), blk), _const_spec((1, D_MODEL)),
                  _const_spec((D_MODEL, 2 * MEM_WIDTH)), _const_spec((1, LANES))],
        out_specs=[pl.BlockSpec((1, N_MEM, MEM_WIDTH), blk), pl.BlockSpec((1, N_MEM, MEM_WIDTH), blk)],
        out_shape=[jax.ShapeDtypeStruct((b, N_MEM, MEM_WIDTH), BF16)] * 2,
        compiler_params=_params(("parallel",)), name="mem_kv",
    )(mem, w["memn"], w["wmemkv"], w["mkg"])


def _attention_kernel(sink_ref, q_ref, k_ref, v_ref, mq_ref, mk_ref, mv_ref, o_ref, mo_ref, *, seq):
    qi = pl.program_id(1)
    tq = q_ref.shape[1]
    kwin = 3 * BLOCK

    lane_k = lax.broadcasted_iota(jnp.int32, (kwin, 2 * LANES), 1)
    lane_o = lax.broadcasted_iota(jnp.int32, (BLOCK, 2 * LANES), 1)
    row_i = lax.broadcasted_iota(jnp.int32, (BLOCK, kwin), 0)
    col_i = lax.broadcasted_iota(jnp.int32, (BLOCK, kwin), 1)

    windows = {}

    def window(blk):
        if blk not in windows:
            r0 = qi * tq + blk * BLOCK
            ks = pl.multiple_of(jnp.clip(r0 - BLOCK, 0, seq - kwin), BLOCK)
            kw = k_ref[0, pl.ds(ks, kwin), :]
            vw = v_ref[0, pl.ds(ks, kwin), :]
            delta = (col_i - row_i) + (ks - r0)
            bias = jnp.where(jnp.abs(delta) <= WINDOW, 0.0, NEG_BIG).astype(F32)
            zero = jnp.zeros_like(kw)
            kj = [jnp.where((lane_k % LANES) // HALF == j, kw, zero) for j in range(N_KV_HEADS)]
            kpairs = [jnp.concatenate(kj[2 * i:2 * i + 2], axis=0) for i in range(N_KV_HEADS // 2)]
            v4 = jnp.concatenate([jnp.where(lane_k // HEAD_DIM == j, vw, zero) for j in range(N_KV_HEADS)], axis=0)
            windows[blk] = (bias, kpairs, v4)
        return windows[blk]

    def logits(item):
        if item[0] == "w":
            _, blk, p = item
            _, kpairs, _ = window(blk)
            qp = q_ref[0, blk * BLOCK:(blk + 1) * BLOCK, p * 256:(p + 1) * 256]
            pairs = [_dot_nt(qp, kp) for kp in kpairs]
            return [pr[:, i * kwin:(i + 1) * kwin] for pr in pairs for i in range(2)]
        hh = item[1]
        return _dot_nt(mq_ref[0, :, hh * LANES:(hh + 1) * LANES], mk_ref[0, :, hh * LANES:(hh + 1) * LANES])

    def softmax(item, sc):
        if item[0] == "w":
            _, blk, p = item
            bias, _, _ = window(blk)
            probs, invs = [], []
            for j in range(N_KV_HEADS):
                snk = sink_ref[j * GQA_GROUP + p] * LOG2E
                s = sc[j] + bias
                m = jnp.maximum(jnp.max(s, axis=-1, keepdims=True), snk)
                e = jnp.exp2(s - m)
                den = jnp.sum(e, axis=-1, keepdims=True) + jnp.exp2(snk - m)
                probs.append(e.astype(BF16))
                invs.append(1.0 / den)
            inv = jnp.where(lane_o < HEAD_DIM, invs[0],
                            jnp.where(lane_o < 2 * HEAD_DIM, invs[1],
                                      jnp.where(lane_o < 3 * HEAD_DIM, invs[2], invs[3])))
            return jnp.concatenate(probs, axis=1), inv
        m = jnp.max(sc, axis=-1, keepdims=True)
        e = jnp.exp2(sc - m)
        return e.astype(BF16), 1.0 / jnp.sum(e, axis=-1, keepdims=True)

    def values(item, pr):
        probs, inv = pr
        if item[0] == "w":
            _, blk, p = item
            _, _, v4 = window(blk)
            o_ref[0, blk * BLOCK:(blk + 1) * BLOCK, p * 256:(p + 1) * 256] = (_dot(probs, v4) * inv).astype(BF16)
        else:
            hh = item[1]
            o = _dot(probs, mv_ref[0, :, hh * LANES:(hh + 1) * LANES])
            mo_ref[0, :, hh * LANES:(hh + 1) * LANES] = (o * inv).astype(BF16)

    items = [("w", blk, p) for blk in range(tq // BLOCK) for p in range(GQA_GROUP)]
    items += [("m", hh) for hh in range(MEM_HEADS)]
    n = len(items)
    sc_next = logits(items[0])
    pr_prev = None
    for i in range(n):
        sc_cur = sc_next
        if i + 1 < n:
            sc_next = logits(items[i + 1])
        if pr_prev is not None:
            values(items[i - 1], pr_prev)
        pr_prev = softmax(items[i], sc_cur)
    values(items[n - 1], pr_prev)


def _attention(q, k, v, mq, mk, mv, sink):
    b, s, _ = q.shape
    tq = TQ_ATTN
    qblk = lambda bi, i: (bi, i, 0)
    full = lambda bi, i: (bi, 0, 0)
    return pl.pallas_call(
        functools.partial(_attention_kernel, seq=s), grid=(b, s // tq),
        in_specs=[pl.BlockSpec(memory_space=pltpu.SMEM),
                  pl.BlockSpec((1, tq, ATTN_WIDTH), qblk),
                  pl.BlockSpec((1, s, KV_WIDTH), full),
                  pl.BlockSpec((1, s, KV_WIDTH), full),
                  pl.BlockSpec((1, tq, MEM_WIDTH), qblk),
                  pl.BlockSpec((1, N_MEM, MEM_WIDTH), full),
                  pl.BlockSpec((1, N_MEM, MEM_WIDTH), full)],
        out_specs=[pl.BlockSpec((1, tq, ATTN_WIDTH), qblk), pl.BlockSpec((1, tq, MEM_WIDTH), qblk)],
        out_shape=[jax.ShapeDtypeStruct((b, s, ATTN_WIDTH), BF16), jax.ShapeDtypeStruct((b, s, MEM_WIDTH), BF16)],
        compiler_params=_params(("parallel", "arbitrary")), name="attention",
    )(sink, q, k, v, mq, mk, mv)


def _ssd_conv_kernel(x_ref, prev_ref, next_ref, w_ref, b_ref, o_ref):
    ci = pl.program_id(1)
    nci = pl.num_programs(1)
    tc = x_ref.shape[1]
    hb = HALO_BF16
    pad = SSM_CONV // 2
    xe = jnp.concatenate([prev_ref[0].astype(F32) * (ci > 0).astype(F32),
                          x_ref[0].astype(F32),
                          next_ref[0].astype(F32) * (ci < nci - 1).astype(F32)], axis=0)
    y = b_ref[...]
    for t in range(SSM_CONV):
        y = y + _shift_rows(xe, hb, tc, t - pad) * w_ref[t:t + 1, :]
    o_ref[0] = _silu(y).astype(BF16)


def _ssd_conv(xbc, w):
    b, s, _ = xbc.shape
    tc = TC_CONV
    r = tc // HALO_BF16
    nh = s // HALO_BF16
    return pl.pallas_call(
        _ssd_conv_kernel, grid=(b, s // tc),
        in_specs=[pl.BlockSpec((1, tc, XBC_WIDTH), lambda bi, i: (bi, i, 0)),
                  pl.BlockSpec((1, HALO_BF16, XBC_WIDTH), lambda bi, i: (bi, jnp.maximum(i * r - 1, 0), 0)),
                  pl.BlockSpec((1, HALO_BF16, XBC_WIDTH), lambda bi, i: (bi, jnp.minimum((i + 1) * r, nh - 1), 0)),
                  _const_spec((SSM_CONV, XBC_WIDTH)), _const_spec((1, XBC_WIDTH))],
        out_specs=pl.BlockSpec((1, tc, XBC_WIDTH), lambda bi, i: (bi, i, 0)),
        out_shape=jax.ShapeDtypeStruct((b, s, XBC_WIDTH), BF16),
        compiler_params=_params(("parallel", "parallel")), name="ssd_conv",
    )(xbc, xbc, xbc, w["convw"], w["convb"])


def _slab(rows, sel):
    hi = rows.astype(BF16)
    lo = (rows - hi.astype(F32)).astype(BF16)
    return _dot_tn(jnp.concatenate([hi, lo], axis=0), sel)


def _ssd_scan_kernel(xf_ref, xb_ref, dttf_ref, dttb_ref, bias_c_ref, alog_c_ref, dskip_ref,
                     onehot_ref, self_ref, selb_ref, ya_ref, yb_ref, hf_ref, hb_ref):
    c = pl.program_id(1)

    @pl.when(c == 0)
    def _():
        hf_ref[...] = jnp.zeros_like(hf_ref)
        hb_ref[...] = jnp.zeros_like(hb_ref)

    ln = SSM_CHUNK
    ri = lax.broadcasted_iota(jnp.int32, (ln, ln), 0)
    cj = lax.broadcasted_iota(jnp.int32, (ln, ln), 1)
    low_incl = (cj <= ri)
    tri_l = jnp.where(low_incl, 1.0, 0.0).astype(BF16)
    tri_u = jnp.where(cj >= ri, 1.0, 0.0).astype(BF16)

    a_col = -jnp.exp(alog_c_ref[...])

    def exact_right(a, m):
        hi, mid, lo = _split3(a)
        return _dot(hi, m) + _dot(mid, m) + _dot(lo, m)

    def carry(x_chunk, slab_e, slab_w, edge, h_ref):
        x32 = x_chunk[:, :SSM_INNER].astype(F32)
        outs = []
        for g in range(SSM_GROUPS):
            lo_, hi_ = g * GROUP_INNER, (g + 1) * GROUP_INNER
            bm = x_chunk[:, SSM_INNER + g * SSM_STATE:SSM_INNER + (g + 1) * SSM_STATE]
            cm = x_chunk[:, SSM_INNER + SSM_GROUPS * SSM_STATE + g * SSM_STATE:SSM_INNER + SSM_GROUPS * SSM_STATE + (g + 1) * SSM_STATE]
            hprev = h_ref[g]
            outs.append(_dot(cm, hprev.astype(BF16)) * slab_e[:, lo_:hi_])
            xw = (x32[:, lo_:hi_] * slab_w[:, lo_:hi_]).astype(BF16)
            h_ref[g] = hprev * slab_e[edge:edge + 1, lo_:hi_] + _dot_tn(bm, xw)
        return jnp.concatenate(outs, axis=1)

    lt = cj < ri
    gt = cj > ri
    row16 = lax.broadcasted_iota(jnp.int32, (DT_WIDTH, ln), 0)
    trow = lax.broadcasted_iota(jnp.int32, (DT_WIDTH, 2 * ln), 0)
    lane = lax.broadcasted_iota(jnp.int32, (ln, LANES), 1)

    n_sub = xf_ref.shape[1] // ln

    def b_m(x_chunk, g):
        return x_chunk[:, SSM_INNER + g * SSM_STATE:SSM_INNER + (g + 1) * SSM_STATE]

    def c_m(x_chunk, g):
        o = SSM_INNER + SSM_GROUPS * SSM_STATE
        return x_chunk[:, o + g * SSM_STATE:o + (g + 1) * SSM_STATE]

    def prepare(k):
        kb = n_sub - 1 - k
        xc = xf_ref[0, k * ln:(k + 1) * ln, :]
        dt_r = _softplus(dttf_ref[:, k * ln:(k + 1) * ln] + bias_c_ref[...])
        a_r = dt_r * a_col
        cs_r = exact_right(a_r, tri_u)
        sf_r = exact_right(a_r, tri_l)
        dt_b = _softplus(dttb_ref[:, kb * ln:(kb + 1) * ln] + bias_c_ref[...])
        sf_b = exact_right(dt_b * a_col, tri_l)
        last = cs_r[:, ln - 1:ln]
        head = sf_b[:, 0:1]

        x_hi, x_mid, x_lo = _split3(jnp.where(row16 < SSM_HEADS, cs_r, sf_r))
        p_mat = jnp.concatenate([x_hi, x_mid, x_lo, jnp.ones((DT_WIDTH, ln), BF16)], axis=0)
        x_terms = [t.astype(F32) for t in (x_hi, x_mid, x_lo)]
        dds = []
        for hd in range(SSM_HEADS):
            t = jnp.zeros((DT_WIDTH, 2 * ln), F32)
            for term, x in enumerate(x_terms):
                piece = jnp.concatenate([x[hd:hd + 1, :], x[SSM_HEADS + hd:SSM_HEADS + hd + 1, :]], axis=1)
                t = jnp.where(trow == term, -piece, t)
            q_mat = jnp.concatenate([onehot_ref[hd], t.astype(BF16)], axis=0)
            dds.append(_dot_tn(p_mat, q_mat))
        return dict(
            xc=xc, dt=dt_r, dds=dds,
            cb=[_dot_nt(c_m(xc, g), b_m(xc, g)) for g in range(SSM_GROUPS)],
            f_e=_slab(jnp.exp(cs_r), self_ref[...]), f_w=_slab(dt_r * jnp.exp(last - cs_r), self_ref[...]),
            b_e=_slab(jnp.exp(sf_b), selb_ref[...]), b_w=_slab(dt_b * jnp.exp(head - sf_b), selb_ref[...]))

    def finish(k, p):
        kb = n_sub - 1 - k
        xc, dt_r = p["xc"], p["dt"]
        xs = xc[:, :SSM_INNER]
        y_parts = []
        for g in range(SSM_GROUPS):
            for pair in range(HEADS_PER_STATE_GROUP // 2):
                ms = []
                for e in range(2):
                    hd = g * HEADS_PER_STATE_GROUP + pair * 2 + e
                    hb_i = SSM_HEADS + hd
                    dd = p["dds"][hd]
                    ex = jnp.exp(jnp.where(low_incl, dd[:, :ln], dd[:, ln:]))
                    dtf_j = dt_r[hd:hd + 1, :]
                    dtb_j = dt_r[hb_i:hb_i + 1, :]
                    dts = jnp.where(lt, dtf_j, jnp.where(gt, dtb_j, dtf_j + dtb_j))
                    ms.append((p["cb"][g] * ex * dts).astype(BF16))
                col0 = (g * HEADS_PER_STATE_GROUP + pair * 2) * SSM_HEAD_DIM
                xp = xs[:, col0:col0 + LANES]
                zero = jnp.zeros_like(xp)
                xbd = jnp.concatenate([jnp.where(lane < SSM_HEAD_DIM, xp, zero),
                                       jnp.where(lane >= SSM_HEAD_DIM, xp, zero)], axis=0)
                y_parts.append(_dot(jnp.concatenate(ms, axis=1), xbd))
        y = jnp.concatenate(y_parts, axis=1) + xs.astype(F32) * dskip_ref[...]
        ya_ref[0, k * ln:(k + 1) * ln, :] = (y + carry(xc, p["f_e"], p["f_w"], ln - 1, hf_ref)).astype(BF16)
        yb_ref[0, kb * ln:(kb + 1) * ln, :] = carry(xb_ref[0, kb * ln:(kb + 1) * ln, :], p["b_e"], p["b_w"], 0,
                                                    hb_ref).astype(BF16)

    nxt = prepare(0)
    for k in range(n_sub):
        cur = nxt
        if k + 1 < n_sub:
            nxt = prepare(k + 1)
        finish(k, cur)


def _ssd_scan(xc, dtt, w):
    b, s, _ = xc.shape
    ts = TS_SCAN
    nc = s // ts
    fwd = lambda bi, c: (bi, c, 0)
    bwd = lambda bi, c: (bi, nc - 1 - c, 0)
    return pl.pallas_call(
        _ssd_scan_kernel, grid=(b, nc),
        in_specs=[pl.BlockSpec((1, ts, XBC_WIDTH), fwd),
                  pl.BlockSpec((1, ts, XBC_WIDTH), bwd),
                  pl.BlockSpec((DT_WIDTH, ts), lambda bi, c: (0, bi * nc + c)),
                  pl.BlockSpec((DT_WIDTH, ts), lambda bi, c: (0, bi * nc + nc - 1 - c)),
                  _const_spec((DT_WIDTH, 1)), _const_spec((DT_WIDTH, 1)),
                  _const_spec((1, SSM_INNER)),
                  _const_spec((SSM_HEADS, 3 * DT_WIDTH, 2 * SSM_CHUNK)),
                  _const_spec((2 * DT_WIDTH, SSM_INNER)), _const_spec((2 * DT_WIDTH, SSM_INNER))],
        out_specs=[pl.BlockSpec((1, ts, SSM_INNER), fwd), pl.BlockSpec((1, ts, SSM_INNER), bwd)],
        out_shape=[jax.ShapeDtypeStruct((b, s, SSM_INNER), BF16)] * 2,
        scratch_shapes=[pltpu.VMEM((SSM_GROUPS, SSM_STATE, GROUP_INNER), F32)] * 2,
        compiler_params=_params(("parallel", "arbitrary")), name="ssd_scan",
    )(xc, xc, dtt, dtt, w["dtb_c"], w["alog_c"], w["dskip"], w["onehot"], w["sel_f"], w["sel_b"])


def _out_proj_kernel(x_ref, attn_ref, ya_ref, yb_ref, z_ref, mem_ref, sn_ref, wa_ref, ws_ref, wm_ref, o_ref):
    y = (ya_ref[...].astype(F32) + yb_ref[...].astype(F32)) * _silu(z_ref[...].astype(F32))
    ms = jnp.mean(y * y, axis=-1, keepdims=True)
    ssm = (y * lax.rsqrt(ms + EPS) * sn_ref[...]).astype(BF16)
    acc = _dot(attn_ref[...], wa_ref[...]) + _dot(ssm, ws_ref[...]) + _dot(mem_ref[...], wm_ref[...])
    o_ref[...] = x_ref[...] + acc


def _out_proj(x2d, attn, ya, yb, z, memo, w):
    t = x2d.shape[0]
    tm = TM_OUT
    row = lambda i: (i, 0)
    return pl.pallas_call(
        _out_proj_kernel, grid=(t // tm,),
        in_specs=[pl.BlockSpec((tm, D_MODEL), row), pl.BlockSpec((tm, ATTN_WIDTH), row),
                  pl.BlockSpec((tm, SSM_INNER), row), pl.BlockSpec((tm, SSM_INNER), row),
                  pl.BlockSpec((tm, SSM_INNER), row), pl.BlockSpec((tm, MEM_WIDTH), row),
                  _const_spec((1, SSM_INNER)), _const_spec((ATTN_WIDTH, D_MODEL)),
                  _const_spec((SSM_INNER, D_MODEL)), _const_spec((MEM_WIDTH, D_MODEL))],
        out_specs=pl.BlockSpec((tm, D_MODEL), row),
        out_shape=jax.ShapeDtypeStruct((t, D_MODEL), F32),
        compiler_params=_params(("parallel",)), name="out_proj",
    )(x2d, attn, ya, yb, z, memo, w["ssmn"], w["wo_a"], w["wo_s"], w["wo_m"])


def _ffn_kernel(x_ref, prev_ref, next_ref, n2_ref, wg_ref, wu_ref, wd_ref, cw_ref, cb_ref, o_ref,
                g_ref, u_ref, act_ref, acc_ref):
    i = pl.program_id(1)
    ni = pl.num_programs(1)
    tm = x_ref.shape[1]
    n2 = n2_ref[...]

    def norm(v):
        ms = jnp.mean(v * v, axis=-1, keepdims=True)
        return (v * lax.rsqrt(ms + EPS) * n2).astype(BF16)

    keep_prev = (i > 0).astype(F32)
    keep_next = (i < ni - 1).astype(F32)
    hl = HALO_BF16
    pad = FFN_CONV // 2

    h = jnp.concatenate([norm(prev_ref[0]), norm(x_ref[0]), norm(next_ref[0])], axis=0)
    n_chunks = D_FF // FF_CHUNK

    def up(cix):
        c0 = cix * FF_CHUNK
        slot = cix % 2
        for dst, wref in ((g_ref, wg_ref), (u_ref, wu_ref)):
            r = _dot(h, wref[:, c0:c0 + FF_CHUNK])
            dst[slot, 0:hl, :] = r[0:hl, :] * keep_prev
            dst[slot, hl:hl + tm, :] = r[hl:hl + tm, :]
            dst[slot, hl + tm:, :] = r[hl + tm:, :] * keep_next

    def conv_act(cix):
        c0 = cix * FF_CHUNK
        slot = cix % 2
        outs = []
        for src, off in ((g_ref, 0), (u_ref, D_FF)):
            cw = cw_ref[:, off + c0:off + c0 + FF_CHUNK]
            cb = cb_ref[:, off + c0:off + c0 + FF_CHUNK]
            xe = src[slot]
            y = cb
            for t in range(FFN_CONV):
                y = y + _shift_rows(xe, hl, tm, t - pad) * cw[t:t + 1, :]
            outs.append(y)
        act_ref[cix % ACT_SLOTS] = (_silu(outs[0]) * outs[1]).astype(BF16)

    def down(cix):
        c0 = cix * FF_CHUNK
        part = _dot(act_ref[cix % ACT_SLOTS], wd_ref[c0:c0 + FF_CHUNK, :])
        if cix == 0:
            acc_ref[...] = part
        else:
            acc_ref[...] += part

    lag = ACT_SLOTS - 1
    up(0)
    for cix in range(n_chunks):
        if cix + 1 < n_chunks:
            up(cix + 1)
        if cix >= lag:
            down(cix - lag)
        conv_act(cix)
    for cix in range(n_chunks - lag, n_chunks):
        down(cix)
    o_ref[0] = x_ref[0] + acc_ref[...]


def _ffn(x1, w):
    b, s, _ = x1.shape
    tm = TM_FFN
    hl = HALO_BF16
    r = tm // hl
    nh = s // hl
    return pl.pallas_call(
        _ffn_kernel, grid=(b, s // tm),
        in_specs=[pl.BlockSpec((1, tm, D_MODEL), lambda bi, i: (bi, i, 0)),
                  pl.BlockSpec((1, hl, D_MODEL), lambda bi, i: (bi, jnp.maximum(i * r - 1, 0), 0)),
                  pl.BlockSpec((1, hl, D_MODEL), lambda bi, i: (bi, jnp.minimum((i + 1) * r, nh - 1), 0)),
                  _const_spec((1, D_MODEL)),
                  _const_spec((D_MODEL, D_FF)), _const_spec((D_MODEL, D_FF)), _const_spec((D_FF, D_MODEL)),
                  _const_spec((FFN_CONV, 2 * D_FF)), _const_spec((1, 2 * D_FF))],
        out_specs=pl.BlockSpec((1, tm, D_MODEL), lambda bi, i: (bi, i, 0)),
        out_shape=jax.ShapeDtypeStruct((b, s, D_MODEL), F32),
        scratch_shapes=[pltpu.VMEM((2, tm + 2 * hl, FF_CHUNK), F32),
                        pltpu.VMEM((2, tm + 2 * hl, FF_CHUNK), F32),
                        pltpu.VMEM((ACT_SLOTS, tm, FF_CHUNK), BF16),
                        pltpu.VMEM((tm, D_MODEL), F32)],
        compiler_params=_params(("parallel", "parallel")), name="ffn",
    )(x1, x1, x1, w["n2"], w["wg"], w["wu"], w["wd"], w["fcw"], w["fcb"])


def _prep_weights(max_seq, norm1_w, w_in, q_norm_w, k_norm_w, attn_sink, ssm_conv_w, ssm_conv_b, ssm_dt_bias,
                  ssm_A_log, ssm_D, ssm_norm_w, mem_norm_w, w_mem_kv, mq_norm_w, mk_norm_w, w_out, norm2_w,
                  w_ffn_up, ffn_conv_w, ffn_conv_b, w_ffn_down):
    o = 0
    wq = w_in[:, o:o + ATTN_WIDTH]; o += ATTN_WIDTH
    wk = w_in[:, o:o + KV_WIDTH]; o += KV_WIDTH
    wv = w_in[:, o:o + KV_WIDTH]; o += KV_WIDTH
    wz = w_in[:, o:o + SSM_INNER]; o += SSM_INNER
    wx = w_in[:, o:o + XBC_WIDTH]; o += XBC_WIDTH
    wdt = w_in[:, o:o + DT_WIDTH]; o += DT_WIDTH
    wmq = w_in[:, o:o + MEM_WIDTH]

    wq_p = wq.reshape(D_MODEL, N_KV_HEADS, GQA_GROUP, 2, HALF).transpose(0, 2, 3, 1, 4).reshape(D_MODEL, ATTN_WIDTH)
    wk_p = wk.reshape(D_MODEL, N_KV_HEADS, 2, HALF).transpose(0, 2, 1, 3).reshape(D_MODEL, KV_WIDTH)
    qg = jnp.broadcast_to(q_norm_w.reshape(1, 2, 1, HALF), (GQA_GROUP, 2, N_KV_HEADS, HALF)).reshape(ATTN_WIDTH)
    kg = jnp.broadcast_to(k_norm_w.reshape(2, 1, HALF), (2, N_KV_HEADS, HALF)).reshape(KV_WIDTH)
    qkg = jnp.concatenate([qg * (HEAD_DIM ** -0.5 * LOG2E), kg]).reshape(1, QK_WIDTH)

    cq = np.arange(ATTN_WIDTH)
    eq = (cq // 256) * N_KV_HEADS + (cq % LANES) // HALF
    ck = np.arange(KV_WIDTH)
    ek = N_Q_HEADS + (ck % LANES) // HALF
    e_all = np.concatenate([eq, ek])
    hsum = np.zeros((QK_WIDTH, LANES), np.float32)
    hsum[np.arange(QK_WIDTH), e_all] = 1.0
    hexp = np.concatenate([hsum.T, hsum.T], axis=0)

    inv = ROPE_THETA ** (-jnp.arange(0, HEAD_DIM, 2, dtype=F32) / HEAD_DIM)
    ang = jnp.arange(max_seq, dtype=F32)[:, None] * inv[None, :]
    cos = jnp.tile(jnp.cos(ang), (1, LANES // HALF))
    sin = jnp.tile(jnp.sin(ang), (1, LANES // HALF))

    wo_a = w_out[:ATTN_WIDTH].reshape(N_KV_HEADS, GQA_GROUP, HEAD_DIM, D_MODEL).transpose(1, 0, 2, 3)
    wo_a = wo_a.reshape(ATTN_WIDTH, D_MODEL)

    r48 = np.arange(3 * DT_WIDTH) % DT_WIDTH
    onehot = np.zeros((SSM_HEADS, 3 * DT_WIDTH, 2 * SSM_CHUNK), np.float32)
    for hd in range(SSM_HEADS):
        onehot[hd, r48 == hd, :SSM_CHUNK] = 1.0
        onehot[hd, r48 == SSM_HEADS + hd, SSM_CHUNK:] = 1.0
    r32 = np.arange(2 * DT_WIDTH) % DT_WIDTH
    col_head = np.arange(SSM_INNER) // SSM_HEAD_DIM
    sel_f = (r32[:, None] == col_head[None, :]).astype(np.float32)
    sel_b = (r32[:, None] == SSM_HEADS + col_head[None, :]).astype(np.float32)
    return {
        "onehot": jnp.asarray(onehot, BF16), "sel_f": jnp.asarray(sel_f, BF16), "sel_b": jnp.asarray(sel_b, BF16),
        "n1": norm1_w.reshape(1, D_MODEL),
        "wqk": jnp.concatenate([wq_p, wk_p], axis=1).astype(BF16),
        "wv": wv.astype(BF16), "wz": wz.astype(BF16), "wx": wx.astype(BF16),
        "wdtt": wdt.T.astype(BF16), "wmq": wmq.astype(BF16),
        "hsum": jnp.asarray(hsum, BF16), "hexp": jnp.asarray(hexp, BF16),
        "qkg": qkg, "mqg": mq_norm_w.reshape(1, LANES), "cos": cos, "sin": sin,
        "sink": attn_sink,
        "memn": mem_norm_w.reshape(1, D_MODEL), "wmemkv": w_mem_kv.astype(BF16), "mkg": mk_norm_w.reshape(1, LANES),
        "convw": ssm_conv_w, "convb": ssm_conv_b.reshape(1, XBC_WIDTH),
        "dtb_c": ssm_dt_bias.reshape(DT_WIDTH, 1), "alog_c": ssm_A_log.reshape(DT_WIDTH, 1),
        "dskip": jnp.repeat(ssm_D, SSM_HEAD_DIM).reshape(1, SSM_INNER),
        "ssmn": ssm_norm_w.reshape(1, SSM_INNER),
        "wo_a": wo_a.astype(BF16),
        "wo_s": w_out[ATTN_WIDTH:ATTN_WIDTH + SSM_INNER].astype(BF16),
        "wo_m": w_out[ATTN_WIDTH + SSM_INNER:].astype(BF16),
        "n2": norm2_w.reshape(1, D_MODEL),
        "wg": w_ffn_up[:, :D_FF].astype(BF16), "wu": w_ffn_up[:, D_FF:].astype(BF16),
        "wd": w_ffn_down.astype(BF16),
        "fcw": ffn_conv_w, "fcb": ffn_conv_b.reshape(1, 2 * D_FF),
    }


def _encoder_layer(x, mem, w):
    b, s, _ = x.shape
    t = b * s
    x2d = x.reshape(t, D_MODEL)
    q, k, v, z, xbc, dtt, mq = _in_proj(x2d, s, w)
    mk, mv = _mem_kv(mem, w)
    attn, memo = _attention(q.reshape(b, s, -1), k.reshape(b, s, -1), v.reshape(b, s, -1),
                            mq.reshape(b, s, -1), mk, mv, w["sink"])
    xc = _ssd_conv(xbc.reshape(b, s, -1), w)
    ya, yb = _ssd_scan(xc, dtt, w)
    x1 = _out_proj(x2d, attn.reshape(t, -1), ya.reshape(t, -1), yb.reshape(t, -1), z, memo.reshape(t, -1), w)
    return _ffn(x1.reshape(b, s, D_MODEL), w)


def kernel(x_prompt, x_sample, mem_prompt, mem_sample, norm1_w, w_in, q_norm_w, k_norm_w, attn_sink, ssm_conv_w, ssm_conv_b, ssm_dt_bias, ssm_A_log, ssm_D, ssm_norm_w, mem_norm_w, w_mem_kv, mq_norm_w, mk_norm_w, w_out, norm2_w, w_ffn_up, ffn_conv_w, ffn_conv_b, w_ffn_down):
    weights = (norm1_w, w_in, q_norm_w, k_norm_w, attn_sink, ssm_conv_w, ssm_conv_b, ssm_dt_bias,
               ssm_A_log, ssm_D, ssm_norm_w, mem_norm_w, w_mem_kv, mq_norm_w, mk_norm_w, w_out,
               norm2_w, w_ffn_up, ffn_conv_w, ffn_conv_b, w_ffn_down)
    depth = norm1_w.shape[0]
    max_seq = max(x_prompt.shape[1], x_sample.shape[1])
    y_prompt, y_sample = x_prompt, x_sample
    for layer in range(depth):
        w = _prep_weights(max_seq, *[p[layer] for p in weights])
        y_prompt = _encoder_layer(y_prompt, mem_prompt, w)
        y_sample = _encoder_layer(y_sample, mem_sample, w)
    return (y_prompt, y_sample)
```

```python
import functools
import math

import numpy as np
import jax
import jax.numpy as jnp
from jax import lax
from jax.experimental import pallas as pl
from jax.experimental.pallas import tpu as pltpu

F32 = jnp.float32
BF16 = jnp.bfloat16

D_MODEL = 1024
HEAD_DIM = 64
HALF = HEAD_DIM // 2
N_Q_HEADS = 16
N_KV_HEADS = 4
GQA_GROUP = N_Q_HEADS // N_KV_HEADS
ATTN_WIDTH = N_Q_HEADS * HEAD_DIM
KV_WIDTH = N_KV_HEADS * HEAD_DIM
WINDOW = 128
BLOCK = 128
ROPE_THETA = 10000.0
SSM_HEADS = 8
SSM_HEAD_DIM = 64
SSM_INNER = SSM_HEADS * SSM_HEAD_DIM
SSM_GROUPS = 2
SSM_STATE = 128
SSM_CONV = 5
SSM_CHUNK = 128
XBC_WIDTH = SSM_INNER + 2 * SSM_GROUPS * SSM_STATE
DT_WIDTH = 2 * SSM_HEADS
N_MEM = 256
MEM_HEADS = 4
MEM_HEAD_DIM = 128
MEM_WIDTH = MEM_HEADS * MEM_HEAD_DIM
D_FF = 2816
FFN_CONV = 3
EPS = 1e-6

LANES = 128
QK_WIDTH = ATTN_WIDTH + KV_WIDTH
HEADS_PER_STATE_GROUP = SSM_HEADS // SSM_GROUPS
GROUP_INNER = HEADS_PER_STATE_GROUP * SSM_HEAD_DIM
NEG_BIG = -1e30
LOG2E = math.log2(math.e)
VMEM_LIMIT = 56 * 1024 * 1024

TM_PROJ = 512
TQ_ATTN = 512
TC_CONV = 512
TS_SCAN = 1024
TM_OUT = 512
TM_FFN = 512
FFN_SUB = 512
FF_CHUNK = 256
ACT_SLOTS = 3
HALO = 8
HALO_BF16 = 16


def _dot(a, b):
    return jnp.dot(a, b, preferred_element_type=F32)


def _dot_nt(a, b):
    return lax.dot_general(a, b, (((1,), (1,)), ((), ())), preferred_element_type=F32)


def _dot_tn(a, b):
    return lax.dot_general(a, b, (((0,), (0,)), ((), ())), preferred_element_type=F32)


def _split3(a):
    hi = a.astype(BF16)
    r = a - hi.astype(F32)
    mid = r.astype(BF16)
    lo = (r - mid.astype(F32)).astype(BF16)
    return hi, mid, lo


def _silu(x):
    return x / (1.0 + jnp.exp(-x))


def _softplus(x):
    return jnp.maximum(x, 0.0) + jnp.log1p(jnp.exp(-jnp.abs(x)))


def _shift_rows(xe, off, n, delta):
    cur = xe[off:off + n]
    if delta == 0:
        return cur
    r = lax.broadcasted_iota(jnp.int32, cur.shape, 0) % HALO
    if delta < 0:
        merged = jnp.where(r < HALO + delta, cur, xe[off - HALO:off + n - HALO])
        rot = -delta
    else:
        merged = jnp.where(r >= delta, cur, xe[off + HALO:off + n + HALO])
        rot = HALO - delta
    width = cur.shape[1]
    return pltpu.roll(merged.reshape(n // HALO, HALO, width), rot, axis=1).reshape(n, width)


def _params(sem):
    return pltpu.CompilerParams(dimension_semantics=sem, vmem_limit_bytes=VMEM_LIMIT)


def _const_spec(shape):
    nd = len(shape)
    return pl.BlockSpec(shape, lambda *_: (0,) * nd)


def _resident_spec(shape):
    nd = len(shape)
    return pl.BlockSpec(shape, lambda *_: (0,) * nd, pipeline_mode=pl.Buffered(1))


def _in_proj_kernel(x_ref, xprev_ref, xnext_ref, n1_ref, wqk_ref, wv_ref, wz_ref, wx_ref, wdtt_ref, wmq_ref,
                    hsum_ref, hexp_ref, qkg_ref, mqg_ref, cos_ref, sin_ref, convw_ref, convb_ref,
                    q_ref, k_ref, v_ref, z_ref, xbc_ref, dtt_ref, mq_ref, *, blocks_per_seq):
    tm = x_ref.shape[0]
    n1 = n1_ref[...]

    def norm(x):
        ms = jnp.mean(x * x, axis=-1, keepdims=True)
        return (x * lax.rsqrt(ms + EPS) * n1).astype(BF16)

    h = norm(x_ref[...])

    pos = pl.program_id(0) % blocks_per_seq
    hb = HALO_BF16
    pad = SSM_CONV // 2
    hext = jnp.concatenate([norm(xprev_ref[...]), h, norm(xnext_ref[...])], axis=0)
    keep_prev = (pos > 0).astype(F32)
    keep_next = (pos < blocks_per_seq - 1).astype(F32)

    def xbc_piece(c0, c1):
        xe = _dot(hext, wx_ref[:, c0:c1])
        xe = jnp.concatenate([xe[:hb] * keep_prev, xe[hb:hb + tm], xe[hb + tm:] * keep_next], axis=0)
        y = convb_ref[:, c0:c1]
        for t in range(SSM_CONV):
            y = y + _shift_rows(xe, hb, tm, t - pad) * convw_ref[t:t + 1, c0:c1]
        xbc_ref[:, c0:c1] = _silu(y).astype(BF16)

    xbc_piece(0, 256)
    v_ref[...] = _dot(h, wv_ref[...]).astype(BF16)
    xbc_piece(256, 512)
    z_ref[...] = _dot(h, wz_ref[...]).astype(BF16)
    xbc_piece(512, 768)
    mq = _dot(h, wmq_ref[...])
    mqg = mqg_ref[...] * (MEM_HEAD_DIM ** -0.5 * LOG2E)
    for hh in range(MEM_HEADS):
        m = mq[:, hh * LANES:(hh + 1) * LANES]
        r = lax.rsqrt(jnp.mean(m * m, axis=-1, keepdims=True) + EPS)
        mq_ref[:, hh * LANES:(hh + 1) * LANES] = (m * r * mqg).astype(BF16)
    xbc_piece(768, 1024)
    dtt_ref[...] = _dot_nt(wdtt_ref[...], h)

    qk = _dot(h, wqk_ref[...])
    ssq = _dot((qk * qk).astype(BF16), hsum_ref[...])
    inv = lax.rsqrt(ssq * (1.0 / HEAD_DIM) + EPS)
    inv_hi = inv.astype(BF16)
    inv_lo = (inv - inv_hi.astype(F32)).astype(BF16)
    scale = _dot(jnp.concatenate([inv_hi, inv_lo], axis=1), hexp_ref[...])
    qkn = qk * scale * qkg_ref[...]

    cos = cos_ref[...]
    sin = sin_ref[...]
    for p in range(GQA_GROUP):
        u = qkn[:, p * 256:p * 256 + LANES]
        w = qkn[:, p * 256 + LANES:(p + 1) * 256]
        q_ref[:, p * 256:p * 256 + LANES] = (u * cos - w * sin).astype(BF16)
        q_ref[:, p * 256 + LANES:(p + 1) * 256] = (w * cos + u * sin).astype(BF16)
    u = qkn[:, ATTN_WIDTH:ATTN_WIDTH + LANES]
    w = qkn[:, ATTN_WIDTH + LANES:QK_WIDTH]
    k_ref[:, :LANES] = (u * cos - w * sin).astype(BF16)
    k_ref[:, LANES:] = (w * cos + u * sin).astype(BF16)


def _in_proj(x2d, s, w):
    t = x2d.shape[0]
    tm = TM_PROJ
    nblk_seq = s // tm
    row = lambda i: (i, 0)
    tab = lambda i: (i % nblk_seq, 0)
    r = tm // HALO_BF16
    nh = t // HALO_BF16
    in_specs = [
        pl.BlockSpec((tm, D_MODEL), row),
        pl.BlockSpec((HALO_BF16, D_MODEL), lambda i: (jnp.maximum(i * r - 1, 0), 0)),
        pl.BlockSpec((HALO_BF16, D_MODEL), lambda i: (jnp.minimum((i + 1) * r, nh - 1), 0)),
        _const_spec((1, D_MODEL)),
        _const_spec((D_MODEL, QK_WIDTH)),
        _const_spec((D_MODEL, KV_WIDTH)),
        _const_spec((D_MODEL, SSM_INNER)),
        _const_spec((D_MODEL, XBC_WIDTH)),
        _const_spec((DT_WIDTH, D_MODEL)),
        _const_spec((D_MODEL, MEM_WIDTH)),
        _const_spec((QK_WIDTH, LANES)),
        _const_spec((2 * LANES, QK_WIDTH)),
        _const_spec((1, QK_WIDTH)),
        _const_spec((1, LANES)),
        pl.BlockSpec((tm, LANES), tab),
        pl.BlockSpec((tm, LANES), tab),
        _const_spec((SSM_CONV, XBC_WIDTH)),
        _const_spec((1, XBC_WIDTH)),
    ]
    out_shape = [
        jax.ShapeDtypeStruct((t, ATTN_WIDTH), BF16),
        jax.ShapeDtypeStruct((t, KV_WIDTH), BF16),
        jax.ShapeDtypeStruct((t, KV_WIDTH), BF16),
        jax.ShapeDtypeStruct((t, SSM_INNER), BF16),
        jax.ShapeDtypeStruct((t, XBC_WIDTH), BF16),
        jax.ShapeDtypeStruct((DT_WIDTH, t), F32),
        jax.ShapeDtypeStruct((t, MEM_WIDTH), BF16),
    ]
    out_specs = [
        pl.BlockSpec((tm, ATTN_WIDTH), row),
        pl.BlockSpec((tm, KV_WIDTH), row),
        pl.BlockSpec((tm, KV_WIDTH), row),
        pl.BlockSpec((tm, SSM_INNER), row),
        pl.BlockSpec((tm, XBC_WIDTH), row),
        pl.BlockSpec((DT_WIDTH, tm), lambda i: (0, i)),
        pl.BlockSpec((tm, MEM_WIDTH), row),
    ]
    return pl.pallas_call(
        functools.partial(_in_proj_kernel, blocks_per_seq=nblk_seq),
        grid=(t // tm,), in_specs=in_specs, out_specs=out_specs, out_shape=out_shape,
        compiler_params=_params(("parallel",)), name="in_proj",
    )(x2d, x2d, x2d, w["n1"], w["wqk"], w["wv"], w["wz"], w["wx"], w["wdtt"], w["wmq"],
      w["hsum"], w["hexp"], w["qkg"], w["mqg"], w["cos"][:s], w["sin"][:s], w["convw"], w["convb"])


def _mem_kv_kernel(mem_ref, nw_ref, w_ref, mkg_ref, mk_ref, mv_ref):
    x = mem_ref[0]
    ms = jnp.mean(x * x, axis=-1, keepdims=True)
    h = (x * lax.rsqrt(ms + EPS) * nw_ref[...]).astype(BF16)
    kv = _dot(h, w_ref[...])
    g = mkg_ref[...]
    for hh in range(MEM_HEADS):
        m = kv[:, hh * LANES:(hh + 1) * LANES]
        r = lax.rsqrt(jnp.mean(m * m, axis=-1, keepdims=True) + EPS)
        mk_ref[0, :, hh * LANES:(hh + 1) * LANES] = (m * r * g).astype(BF16)
    mv_ref[0] = kv[:, MEM_WIDTH:].astype(BF16)


def _mem_kv(mem, w):
    b = mem.shape[0]
    blk = lambda i: (i, 0, 0)
    return pl.pallas_call(
        _mem_kv_kernel, grid=(b,),
        in_specs=[pl.BlockSpec((1, N_MEM, D_MODEL), blk), _const_spec((1, D_MODEL)),
                  _const_spec((D_MODEL, 2 * MEM_WIDTH)), _const_spec((1, LANES))],
        out_specs=[pl.BlockSpec((1, N_MEM, MEM_WIDTH), blk), pl.BlockSpec((1, N_MEM, MEM_WIDTH), blk)],
        out_shape=[jax.ShapeDtypeStruct((b, N_MEM, MEM_WIDTH), BF16)] * 2,
        compiler_params=_params(("parallel",)), name="mem_kv",
    )(mem, w["memn"], w["wmemkv"], w["mkg"])


def _attention_kernel(sink_ref, q_ref, k_ref, v_ref, mq_ref, mk_ref, mv_ref, o_ref, mo_ref, *, seq):
    qi = pl.program_id(1)
    tq = q_ref.shape[1]
    kwin = 3 * BLOCK

    lane_k = lax.broadcasted_iota(jnp.int32, (kwin, 2 * LANES), 1)
    lane_o = lax.broadcasted_iota(jnp.int32, (BLOCK, 2 * LANES), 1)
    row_i = lax.broadcasted_iota(jnp.int32, (BLOCK, kwin), 0)
    col_i = lax.broadcasted_iota(jnp.int32, (BLOCK, kwin), 1)

    windows = {}

    def window(blk):
        if blk not in windows:
            r0 = qi * tq + blk * BLOCK
            ks = pl.multiple_of(jnp.clip(r0 - BLOCK, 0, seq - kwin), BLOCK)
            kw = k_ref[0, pl.ds(ks, kwin), :]
            vw = v_ref[0, pl.ds(ks, kwin), :]
            delta = (col_i - row_i) + (ks - r0)
            bias = jnp.where(jnp.abs(delta) <= WINDOW, 0.0, NEG_BIG).astype(F32)
            zero = jnp.zeros_like(kw)
            kj = [jnp.where((lane_k % LANES) // HALF == j, kw, zero) for j in range(N_KV_HEADS)]
            kpairs = [jnp.concatenate(kj[2 * i:2 * i + 2], axis=0) for i in range(N_KV_HEADS // 2)]
            v4 = jnp.concatenate([jnp.where(lane_k // HEAD_DIM == j, vw, zero) for j in range(N_KV_HEADS)], axis=0)
            windows[blk] = (bias, kpairs, v4)
        return windows[blk]

    def logits(item):
        if item[0] == "w":
            _, blk, p = item
            _, kpairs, _ = window(blk)
            qp = q_ref[0, blk * BLOCK:(blk + 1) * BLOCK, p * 256:(p + 1) * 256]
            pairs = [_dot_nt(qp, kp) for kp in kpairs]
            return [pr[:, i * kwin:(i + 1) * kwin] for pr in pairs for i in range(2)]
        hh = item[1]
        return _dot_nt(mq_ref[0, :, hh * LANES:(hh + 1) * LANES], mk_ref[0, :, hh * LANES:(hh + 1) * LANES])

    def softmax(item, sc):
        if item[0] == "w":
            _, blk, p = item
            bias, _, _ = window(blk)
            probs, invs = [], []
            for j in range(N_KV_HEADS):
                snk = sink_ref[j * GQA_GROUP + p] * LOG2E
                s = sc[j] + bias
                m = jnp.maximum(jnp.max(s, axis=-1, keepdims=True), snk)
                e = jnp.exp2(s - m)
                den = jnp.sum(e, axis=-1, keepdims=True) + jnp.exp2(snk - m)
                probs.append(e.astype(BF16))
                invs.append(1.0 / den)
            inv = jnp.where(lane_o < HEAD_DIM, invs[0],
                            jnp.where(lane_o < 2 * HEAD_DIM, invs[1],
                                      jnp.where(lane_o < 3 * HEAD_DIM, invs[2], invs[3])))
            return jnp.concatenate(probs, axis=1), inv
        m = jnp.max(sc, axis=-1, keepdims=True)
        e = jnp.exp2(sc - m)
        return e.astype(BF16), 1.0 / jnp.sum(e, axis=-1, keepdims=True)

    def values(item, pr):
        probs, inv = pr
        if item[0] == "w":
            _, blk, p = item
            _, _, v4 = window(blk)
            o_ref[0, blk * BLOCK:(blk + 1) * BLOCK, p * 256:(p + 1) * 256] = (_dot(probs, v4) * inv).astype(BF16)
        else:
            hh = item[1]
            o = _dot(probs, mv_ref[0, :, hh * LANES:(hh + 1) * LANES])
            mo_ref[0, :, hh * LANES:(hh + 1) * LANES] = (o * inv).astype(BF16)

    items = [("w", blk, p) for blk in range(tq // BLOCK) for p in range(GQA_GROUP)]
    items += [("m", hh) for hh in range(MEM_HEADS)]
    n = len(items)
    sc_next = logits(items[0])
    pr_prev = None
    for i in range(n):
        sc_cur = sc_next
        if i + 1 < n:
            sc_next = logits(items[i + 1])
        if pr_prev is not None:
            values(items[i - 1], pr_prev)
        pr_prev = softmax(items[i], sc_cur)
    values(items[n - 1], pr_prev)


def _attention(q, k, v, mq, mk, mv, sink):
    b, s, _ = q.shape
    tq = TQ_ATTN
    qblk = lambda bi, i: (bi, i, 0)
    full = lambda bi, i: (bi, 0, 0)
    return pl.pallas_call(
        functools.partial(_attention_kernel, seq=s), grid=(b, s // tq),
        in_specs=[pl.BlockSpec(memory_space=pltpu.SMEM),
                  pl.BlockSpec((1, tq, ATTN_WIDTH), qblk),
                  pl.BlockSpec((1, s, KV_WIDTH), full),
                  pl.BlockSpec((1, s, KV_WIDTH), full),
                  pl.BlockSpec((1, tq, MEM_WIDTH), qblk),
                  pl.BlockSpec((1, N_MEM, MEM_WIDTH), full),
                  pl.BlockSpec((1, N_MEM, MEM_WIDTH), full)],
        out_specs=[pl.BlockSpec((1, tq, ATTN_WIDTH), qblk), pl.BlockSpec((1, tq, MEM_WIDTH), qblk)],
        out_shape=[jax.ShapeDtypeStruct((b, s, ATTN_WIDTH), BF16), jax.ShapeDtypeStruct((b, s, MEM_WIDTH), BF16)],
        compiler_params=_params(("parallel", "arbitrary")), name="attention",
    )(sink, q, k, v, mq, mk, mv)


def _slab(rows, sel):
    hi = rows.astype(BF16)
    lo = (rows - hi.astype(F32)).astype(BF16)
    return _dot_tn(jnp.concatenate([hi, lo], axis=0), sel)


def _ssd_scan_kernel(xf_ref, xb_ref, dttf_ref, dttb_ref, bias_c_ref, alog_c_ref, dskip_ref,
                     onehot_ref, self_ref, selb_ref, ya_ref, yb_ref, hf_ref, hb_ref):
    c = pl.program_id(1)

    @pl.when(c == 0)
    def _():
        hf_ref[...] = jnp.zeros_like(hf_ref)
        hb_ref[...] = jnp.zeros_like(hb_ref)

    ln = SSM_CHUNK
    ri = lax.broadcasted_iota(jnp.int32, (ln, ln), 0)
    cj = lax.broadcasted_iota(jnp.int32, (ln, ln), 1)
    low_incl = (cj <= ri)
    tri_l = jnp.where(low_incl, 1.0, 0.0).astype(BF16)
    tri_u = jnp.where(cj >= ri, 1.0, 0.0).astype(BF16)

    a_col = -jnp.exp(alog_c_ref[...])

    def exact_right(a, m):
        hi, mid, lo = _split3(a)
        return _dot(hi, m) + _dot(mid, m) + _dot(lo, m)

    def carry(x_chunk, slab_e, slab_w, edge, h_ref):
        x32 = x_chunk[:, :SSM_INNER].astype(F32)
        outs = []
        for g in range(SSM_GROUPS):
            lo_, hi_ = g * GROUP_INNER, (g + 1) * GROUP_INNER
            bm = x_chunk[:, SSM_INNER + g * SSM_STATE:SSM_INNER + (g + 1) * SSM_STATE]
            cm = x_chunk[:, SSM_INNER + SSM_GROUPS * SSM_STATE + g * SSM_STATE:SSM_INNER + SSM_GROUPS * SSM_STATE + (g + 1) * SSM_STATE]
            hprev = h_ref[g]
            outs.append(_dot(cm, hprev.astype(BF16)) * slab_e[:, lo_:hi_])
            xw = (x32[:, lo_:hi_] * slab_w[:, lo_:hi_]).astype(BF16)
            h_ref[g] = hprev * slab_e[edge:edge + 1, lo_:hi_] + _dot_tn(bm, xw)
        return jnp.concatenate(outs, axis=1)

    lt = cj < ri
    gt = cj > ri
    row16 = lax.broadcasted_iota(jnp.int32, (DT_WIDTH, ln), 0)
    trow = lax.broadcasted_iota(jnp.int32, (DT_WIDTH, 2 * ln), 0)
    lane = lax.broadcasted_iota(jnp.int32, (ln, LANES), 1)

    n_sub = xf_ref.shape[1] // ln

    def b_m(x_chunk, g):
        return x_chunk[:, SSM_INNER + g * SSM_STATE:SSM_INNER + (g + 1) * SSM_STATE]

    def c_m(x_chunk, g):
        o = SSM_INNER + SSM_GROUPS * SSM_STATE
        return x_chunk[:, o + g * SSM_STATE:o + (g + 1) * SSM_STATE]

    def prepare(k):
        kb = n_sub - 1 - k
        xc = xf_ref[0, k * ln:(k + 1) * ln, :]
        dt_r = _softplus(dttf_ref[:, k * ln:(k + 1) * ln] + bias_c_ref[...])
        a_r = dt_r * a_col
        cs_r = exact_right(a_r, tri_u)
        sf_r = exact_right(a_r, tri_l)
        dt_b = _softplus(dttb_ref[:, kb * ln:(kb + 1) * ln] + bias_c_ref[...])
        sf_b = exact_right(dt_b * a_col, tri_l)
        last = cs_r[:, ln - 1:ln]
        head = sf_b[:, 0:1]

        x_hi, x_mid, x_lo = _split3(jnp.where(row16 < SSM_HEADS, cs_r, sf_r))
        p_mat = jnp.concatenate([x_hi, x_mid, x_lo, jnp.ones((DT_WIDTH, ln), BF16)], axis=0)
        x_terms = [t.astype(F32) for t in (x_hi, x_mid, x_lo)]
        dds = []
        for hd in range(SSM_HEADS):
            t = jnp.zeros((DT_WIDTH, 2 * ln), F32)
            for term, x in enumerate(x_terms):
                piece = jnp.concatenate([x[hd:hd + 1, :], x[SSM_HEADS + hd:SSM_HEADS + hd + 1, :]], axis=1)
                t = jnp.where(trow == term, -piece, t)
            q_mat = jnp.concatenate([onehot_ref[hd], t.astype(BF16)], axis=0)
            dds.append(_dot_tn(p_mat, q_mat))
        return dict(
            xc=xc, dt=dt_r, dds=dds,
            cb=[_dot_nt(c_m(xc, g), b_m(xc, g)) for g in range(SSM_GROUPS)],
            f_e=_slab(jnp.exp(cs_r), self_ref[...]), f_w=_slab(dt_r * jnp.exp(last - cs_r), self_ref[...]),
            b_e=_slab(jnp.exp(sf_b), selb_ref[...]), b_w=_slab(dt_b * jnp.exp(head - sf_b), selb_ref[...]))

    def finish(k, p):
        kb = n_sub - 1 - k
        xc, dt_r = p["xc"], p["dt"]
        xs = xc[:, :SSM_INNER]
        y_parts = []
        for g in range(SSM_GROUPS):
            for pair in range(HEADS_PER_STATE_GROUP // 2):
                ms = []
                for e in range(2):
                    hd = g * HEADS_PER_STATE_GROUP + pair * 2 + e
                    hb_i = SSM_HEADS + hd
                    dd = p["dds"][hd]
                    ex = jnp.exp(jnp.where(low_incl, dd[:, :ln], dd[:, ln:]))
                    dtf_j = dt_r[hd:hd + 1, :]
                    dtb_j = dt_r[hb_i:hb_i + 1, :]
                    dts = jnp.where(lt, dtf_j, jnp.where(gt, dtb_j, dtf_j + dtb_j))
                    ms.append((p["cb"][g] * ex * dts).astype(BF16))
                col0 = (g * HEADS_PER_STATE_GROUP + pair * 2) * SSM_HEAD_DIM
                xp = xs[:, col0:col0 + LANES]
                zero = jnp.zeros_like(xp)
                xbd = jnp.concatenate([jnp.where(lane < SSM_HEAD_DIM, xp, zero),
                                       jnp.where(lane >= SSM_HEAD_DIM, xp, zero)], axis=0)
                y_parts.append(_dot(jnp.concatenate(ms, axis=1), xbd))
        y = jnp.concatenate(y_parts, axis=1) + xs.astype(F32) * dskip_ref[...]
        ya_ref[0, k * ln:(k + 1) * ln, :] = (y + carry(xc, p["f_e"], p["f_w"], ln - 1, hf_ref)).astype(BF16)
        yb_ref[0, kb * ln:(kb + 1) * ln, :] = carry(xb_ref[0, kb * ln:(kb + 1) * ln, :], p["b_e"], p["b_w"], 0,
                                                    hb_ref).astype(BF16)

    nxt = prepare(0)
    for k in range(n_sub):
        cur = nxt
        if k + 1 < n_sub:
            nxt = prepare(k + 1)
        finish(k, cur)


def _ssd_scan(xc, dtt, w):
    b, s, _ = xc.shape
    ts = TS_SCAN
    nc = s // ts
    fwd = lambda bi, c: (bi, c, 0)
    bwd = lambda bi, c: (bi, nc - 1 - c, 0)
    return pl.pallas_call(
        _ssd_scan_kernel, grid=(b, nc),
        in_specs=[pl.BlockSpec((1, ts, XBC_WIDTH), fwd),
                  pl.BlockSpec((1, ts, XBC_WIDTH), bwd),
                  pl.BlockSpec((DT_WIDTH, ts), lambda bi, c: (0, bi * nc + c)),
                  pl.BlockSpec((DT_WIDTH, ts), lambda bi, c: (0, bi * nc + nc - 1 - c)),
                  _const_spec((DT_WIDTH, 1)), _const_spec((DT_WIDTH, 1)),
                  _const_spec((1, SSM_INNER)),
                  _const_spec((SSM_HEADS, 3 * DT_WIDTH, 2 * SSM_CHUNK)),
                  _const_spec((2 * DT_WIDTH, SSM_INNER)), _const_spec((2 * DT_WIDTH, SSM_INNER))],
        out_specs=[pl.BlockSpec((1, ts, SSM_INNER), fwd), pl.BlockSpec((1, ts, SSM_INNER), bwd)],
        out_shape=[jax.ShapeDtypeStruct((b, s, SSM_INNER), BF16)] * 2,
        scratch_shapes=[pltpu.VMEM((SSM_GROUPS, SSM_STATE, GROUP_INNER), F32)] * 2,
        compiler_params=_params(("parallel", "arbitrary")), name="ssd_scan",
    )(xc, xc, dtt, dtt, w["dtb_c"], w["alog_c"], w["dskip"], w["onehot"], w["sel_f"], w["sel_b"])


def _out_proj_kernel(x_ref, attn_ref, ya_ref, yb_ref, z_ref, mem_ref, sn_ref, wa_ref, ws_ref, wm_ref, o_ref):
    y = (ya_ref[...].astype(F32) + yb_ref[...].astype(F32)) * _silu(z_ref[...].astype(F32))
    ms = jnp.mean(y * y, axis=-1, keepdims=True)
    ssm = (y * lax.rsqrt(ms + EPS) * sn_ref[...]).astype(BF16)
    acc = _dot(attn_ref[...], wa_ref[...]) + _dot(ssm, ws_ref[...]) + _dot(mem_ref[...], wm_ref[...])
    o_ref[...] = x_ref[...] + acc


def _out_proj(x2d, attn, ya, yb, z, memo, w):
    t = x2d.shape[0]
    tm = TM_OUT
    row = lambda i: (i, 0)
    return pl.pallas_call(
        _out_proj_kernel, grid=(t // tm,),
        in_specs=[pl.BlockSpec((tm, D_MODEL), row), pl.BlockSpec((tm, ATTN_WIDTH), row),
                  pl.BlockSpec((tm, SSM_INNER), row), pl.BlockSpec((tm, SSM_INNER), row),
                  pl.BlockSpec((tm, SSM_INNER), row), pl.BlockSpec((tm, MEM_WIDTH), row),
                  _const_spec((1, SSM_INNER)), _const_spec((ATTN_WIDTH, D_MODEL)),
                  _const_spec((SSM_INNER, D_MODEL)), _const_spec((MEM_WIDTH, D_MODEL))],
        out_specs=pl.BlockSpec((tm, D_MODEL), row),
        out_shape=jax.ShapeDtypeStruct((t, D_MODEL), F32),
        compiler_params=_params(("parallel",)), name="out_proj",
    )(x2d, attn, ya, yb, z, memo, w["ssmn"], w["wo_a"], w["wo_s"], w["wo_m"])


def _ffn_kernel(x_ref, prev_ref, next_ref, n2_ref, wg_ref, wu_ref, wd_ref, cw_ref, cb_ref, o_ref,
                g_ref, u_ref, act_ref, acc_ref):
    i = pl.program_id(1)
    ni = pl.num_programs(1)
    tm = x_ref.shape[1]
    n2 = n2_ref[...]

    def norm(v):
        ms = jnp.mean(v * v, axis=-1, keepdims=True)
        return (v * lax.rsqrt(ms + EPS) * n2).astype(BF16)

    hl = HALO_BF16
    pad = FFN_CONV // 2
    ts = FFN_SUB
    n_sub = tm // ts
    n_chunks = D_FF // FF_CHUNK

    hs, keeps = [], []
    for sub in range(n_sub):
        lo = norm(prev_ref[0]) if sub == 0 else norm(x_ref[0, sub * ts - hl:sub * ts, :])
        hi = norm(next_ref[0]) if sub == n_sub - 1 else norm(x_ref[0, (sub + 1) * ts:(sub + 1) * ts + hl, :])
        hs.append(jnp.concatenate([lo, norm(x_ref[0, sub * ts:(sub + 1) * ts, :]), hi], axis=0))
        keeps.append(((i > 0).astype(F32) if sub == 0 else None,
                      (i < ni - 1).astype(F32) if sub == n_sub - 1 else None))

    items = [(c, sub) for c in range(n_chunks) for sub in range(n_sub)]

    def up(it):
        c, sub = items[it]
        c0 = c * FF_CHUNK
        slot = it % 2
        keep_prev, keep_next = keeps[sub]
        for dst, wref in ((g_ref, wg_ref), (u_ref, wu_ref)):
            r = _dot(hs[sub], wref[:, c0:c0 + FF_CHUNK])
            dst[slot, 0:hl, :] = r[0:hl, :] if keep_prev is None else r[0:hl, :] * keep_prev
            dst[slot, hl:hl + ts, :] = r[hl:hl + ts, :]
            dst[slot, hl + ts:, :] = r[hl + ts:, :] if keep_next is None else r[hl + ts:, :] * keep_next

    def conv_act(it):
        c, _ = items[it]
        c0 = c * FF_CHUNK
        slot = it % 2
        outs = []
        for src, off in ((g_ref, 0), (u_ref, D_FF)):
            cw = cw_ref[:, off + c0:off + c0 + FF_CHUNK]
            cb = cb_ref[:, off + c0:off + c0 + FF_CHUNK]
            y = cb + src[slot, pl.ds(hl - pad, ts), :] * cw[0:1, :]
            for t in range(1, FFN_CONV):
                y = y + src[slot, pl.ds(hl - pad + t, ts), :] * cw[t:t + 1, :]
            outs.append(y)
        act_ref[it % ACT_SLOTS] = (_silu(outs[0]) * outs[1]).astype(BF16)

    def down(it):
        c, sub = items[it]
        c0 = c * FF_CHUNK
        part = _dot(act_ref[it % ACT_SLOTS], wd_ref[c0:c0 + FF_CHUNK, :])
        if c == 0:
            acc_ref[sub] = part
        else:
            acc_ref[sub] += part

    lag = ACT_SLOTS - 1
    n_items = len(items)
    up(0)
    for it in range(n_items):
        if it + 1 < n_items:
            up(it + 1)
        if it >= lag:
            down(it - lag)
        conv_act(it)
    for it in range(n_items - lag, n_items):
        down(it)
    for sub in range(n_sub):
        o_ref[0, sub * ts:(sub + 1) * ts, :] = x_ref[0, sub * ts:(sub + 1) * ts, :] + acc_ref[sub]


def _ffn(x1, w):
    b, s, _ = x1.shape
    tm = TM_FFN
    hl = HALO_BF16
    r = tm // hl
    nh = s // hl
    return pl.pallas_call(
        _ffn_kernel, grid=(b, s // tm),
        in_specs=[pl.BlockSpec((1, tm, D_MODEL), lambda bi, i: (bi, i, 0)),
                  pl.BlockSpec((1, hl, D_MODEL), lambda bi, i: (bi, jnp.maximum(i * r - 1, 0), 0)),
                  pl.BlockSpec((1, hl, D_MODEL), lambda bi, i: (bi, jnp.minimum((i + 1) * r, nh - 1), 0)),
                  _const_spec((1, D_MODEL)),
                  _resident_spec((D_MODEL, D_FF)), _resident_spec((D_MODEL, D_FF)), _resident_spec((D_FF, D_MODEL)),
                  _const_spec((FFN_CONV, 2 * D_FF)), _const_spec((1, 2 * D_FF))],
        out_specs=pl.BlockSpec((1, tm, D_MODEL), lambda bi, i: (bi, i, 0)),
        out_shape=jax.ShapeDtypeStruct((b, s, D_MODEL), F32),
        scratch_shapes=[pltpu.VMEM((2, FFN_SUB + 2 * hl, FF_CHUNK), F32),
                        pltpu.VMEM((2, FFN_SUB + 2 * hl, FF_CHUNK), F32),
                        pltpu.VMEM((ACT_SLOTS, FFN_SUB, FF_CHUNK), BF16),
                        pltpu.VMEM((tm // FFN_SUB, FFN_SUB, D_MODEL), F32)],
        compiler_params=_params(("parallel", "parallel")), name="ffn",
    )(x1, x1, x1, w["n2"], w["wg"], w["wu"], w["wd"], w["fcw"], w["fcb"])


def _prep_weights(max_seq, norm1_w, w_in, q_norm_w, k_norm_w, attn_sink, ssm_conv_w, ssm_conv_b, ssm_dt_bias,
                  ssm_A_log, ssm_D, ssm_norm_w, mem_norm_w, w_mem_kv, mq_norm_w, mk_norm_w, w_out, norm2_w,
                  w_ffn_up, ffn_conv_w, ffn_conv_b, w_ffn_down):
    o = 0
    wq = w_in[:, o:o + ATTN_WIDTH]; o += ATTN_WIDTH
    wk = w_in[:, o:o + KV_WIDTH]; o += KV_WIDTH
    wv = w_in[:, o:o + KV_WIDTH]; o += KV_WIDTH
    wz = w_in[:, o:o + SSM_INNER]; o += SSM_INNER
    wx = w_in[:, o:o + XBC_WIDTH]; o += XBC_WIDTH
    wdt = w_in[:, o:o + DT_WIDTH]; o += DT_WIDTH
    wmq = w_in[:, o:o + MEM_WIDTH]

    wq_p = wq.reshape(D_MODEL, N_KV_HEADS, GQA_GROUP, 2, HALF).transpose(0, 2, 3, 1, 4).reshape(D_MODEL, ATTN_WIDTH)
    wk_p = wk.reshape(D_MODEL, N_KV_HEADS, 2, HALF).transpose(0, 2, 1, 3).reshape(D_MODEL, KV_WIDTH)
    qg = jnp.broadcast_to(q_norm_w.reshape(1, 2, 1, HALF), (GQA_GROUP, 2, N_KV_HEADS, HALF)).reshape(ATTN_WIDTH)
    kg = jnp.broadcast_to(k_norm_w.reshape(2, 1, HALF), (2, N_KV_HEADS, HALF)).reshape(KV_WIDTH)
    qkg = jnp.concatenate([qg * (HEAD_DIM ** -0.5 * LOG2E), kg]).reshape(1, QK_WIDTH)

    cq = np.arange(ATTN_WIDTH)
    eq = (cq // 256) * N_KV_HEADS + (cq % LANES) // HALF
    ck = np.arange(KV_WIDTH)
    ek = N_Q_HEADS + (ck % LANES) // HALF
    e_all = np.concatenate([eq, ek])
    hsum = np.zeros((QK_WIDTH, LANES), np.float32)
    hsum[np.arange(QK_WIDTH), e_all] = 1.0
    hexp = np.concatenate([hsum.T, hsum.T], axis=0)

    inv = ROPE_THETA ** (-jnp.arange(0, HEAD_DIM, 2, dtype=F32) / HEAD_DIM)
    ang = jnp.arange(max_seq, dtype=F32)[:, None] * inv[None, :]
    cos = jnp.tile(jnp.cos(ang), (1, LANES // HALF))
    sin = jnp.tile(jnp.sin(ang), (1, LANES // HALF))

    wo_a = w_out[:ATTN_WIDTH].reshape(N_KV_HEADS, GQA_GROUP, HEAD_DIM, D_MODEL).transpose(1, 0, 2, 3)
    wo_a = wo_a.reshape(ATTN_WIDTH, D_MODEL)

    r48 = np.arange(3 * DT_WIDTH) % DT_WIDTH
    onehot = np.zeros((SSM_HEADS, 3 * DT_WIDTH, 2 * SSM_CHUNK), np.float32)
    for hd in range(SSM_HEADS):
        onehot[hd, r48 == hd, :SSM_CHUNK] = 1.0
        onehot[hd, r48 == SSM_HEADS + hd, SSM_CHUNK:] = 1.0
    r32 = np.arange(2 * DT_WIDTH) % DT_WIDTH
    col_head = np.arange(SSM_INNER) // SSM_HEAD_DIM
    sel_f = (r32[:, None] == col_head[None, :]).astype(np.float32)
    sel_b = (r32[:, None] == SSM_HEADS + col_head[None, :]).astype(np.float32)
    return {
        "onehot": jnp.asarray(onehot, BF16), "sel_f": jnp.asarray(sel_f, BF16), "sel_b": jnp.asarray(sel_b, BF16),
        "n1": norm1_w.reshape(1, D_MODEL),
        "wqk": jnp.concatenate([wq_p, wk_p], axis=1).astype(BF16),
        "wv": wv.astype(BF16), "wz": wz.astype(BF16), "wx": wx.astype(BF16),
        "wdtt": wdt.T.astype(BF16), "wmq": wmq.astype(BF16),
        "hsum": jnp.asarray(hsum, BF16), "hexp": jnp.asarray(hexp, BF16),
        "qkg": qkg, "mqg": mq_norm_w.reshape(1, LANES), "cos": cos, "sin": sin,
        "sink": attn_sink,
        "memn": mem_norm_w.reshape(1, D_MODEL), "wmemkv": w_mem_kv.astype(BF16), "mkg": mk_norm_w.reshape(1, LANES),
        "convw": ssm_conv_w, "convb": ssm_conv_b.reshape(1, XBC_WIDTH),
        "dtb_c": ssm_dt_bias.reshape(DT_WIDTH, 1), "alog_c": ssm_A_log.reshape(DT_WIDTH, 1),
        "dskip": jnp.repeat(ssm_D, SSM_HEAD_DIM).reshape(1, SSM_INNER),
        "ssmn": ssm_norm_w.reshape(1, SSM_INNER),
        "wo_a": wo_a.astype(BF16),
        "wo_s": w_out[ATTN_WIDTH:ATTN_WIDTH + SSM_INNER].astype(BF16),
        "wo_m": w_out[ATTN_WIDTH + SSM_INNER:].astype(BF16),
        "n2": norm2_w.reshape(1, D_MODEL),
        "wg": w_ffn_up[:, :D_FF].astype(BF16), "wu": w_ffn_up[:, D_FF:].astype(BF16),
        "wd": w_ffn_down.astype(BF16),
        "fcw": ffn_conv_w, "fcb": ffn_conv_b.reshape(1, 2 * D_FF),
    }


def _encoder_layer(x, mem, w):
    b, s, _ = x.shape
    t = b * s
    x2d = x.reshape(t, D_MODEL)
    q, k, v, z, xbc, dtt, mq = _in_proj(x2d, s, w)
    mk, mv = _mem_kv(mem, w)
    attn, memo = _attention(q.reshape(b, s, -1), k.reshape(b, s, -1), v.reshape(b, s, -1),
                            mq.reshape(b, s, -1), mk, mv, w["sink"])
    ya, yb = _ssd_scan(xbc.reshape(b, s, -1), dtt, w)
    x1 = _out_proj(x2d, attn.reshape(t, -1), ya.reshape(t, -1), yb.reshape(t, -1), z, memo.reshape(t, -1), w)
    return _ffn(x1.reshape(b, s, D_MODEL), w)


def kernel(x_prompt, x_sample, mem_prompt, mem_sample, norm1_w, w_in, q_norm_w, k_norm_w, attn_sink, ssm_conv_w, ssm_conv_b, ssm_dt_bias, ssm_A_log, ssm_D, ssm_norm_w, mem_norm_w, w_mem_kv, mq_norm_w, mk_norm_w, w_out, norm2_w, w_ffn_up, ffn_conv_w, ffn_conv_b, w_ffn_down):
    weights = (norm1_w, w_in, q_norm_w, k_norm_w, attn_sink, ssm_conv_w, ssm_conv_b, ssm_dt_bias,
               ssm_A_log, ssm_D, ssm_norm_w, mem_norm_w, w_mem_kv, mq_norm_w, mk_norm_w, w_out,
               norm2_w, w_ffn_up, ffn_conv_w, ffn_conv_b, w_ffn_down)
    depth = norm1_w.shape[0]
    max_seq = max(x_prompt.shape[1], x_sample.shape[1])
    y_prompt, y_sample = x_prompt, x_sample
    for layer in range(depth):
        w = _prep_weights(max_seq, *[p[layer] for p in weights])
        y_prompt = _encoder_layer(y_prompt, mem_prompt, w)
        y_sample = _encoder_layer(y_sample, mem_sample, w)
    return (y_prompt, y_sample)
```

```python
import functools
import math

import numpy as np
import jax
import jax.numpy as jnp
from jax import lax
from jax.experimental import pallas as pl
from jax.experimental.pallas import tpu as pltpu

F32 = jnp.float32
BF16 = jnp.bfloat16

D_MODEL = 1024
HEAD_DIM = 64
HALF = HEAD_DIM // 2
N_Q_HEADS = 16
N_KV_HEADS = 4
GQA_GROUP = N_Q_HEADS // N_KV_HEADS
ATTN_WIDTH = N_Q_HEADS * HEAD_DIM
KV_WIDTH = N_KV_HEADS * HEAD_DIM
WINDOW = 128
BLOCK = 128
ROPE_THETA = 10000.0
SSM_HEADS = 8
SSM_HEAD_DIM = 64
SSM_INNER = SSM_HEADS * SSM_HEAD_DIM
SSM_GROUPS = 2
SSM_STATE = 128
SSM_CONV = 5
SSM_CHUNK = 128
XBC_WIDTH = SSM_INNER + 2 * SSM_GROUPS * SSM_STATE
DT_WIDTH = 2 * SSM_HEADS
N_MEM = 256
MEM_HEADS = 4
MEM_HEAD_DIM = 128
MEM_WIDTH = MEM_HEADS * MEM_HEAD_DIM
MIX_WIDTH = ATTN_WIDTH + SSM_INNER + MEM_WIDTH
D_FF = 2816
FFN_CONV = 3
EPS = 1e-6

LANES = 128
QK_WIDTH = ATTN_WIDTH + KV_WIDTH
HEADS_PER_STATE_GROUP = SSM_HEADS // SSM_GROUPS
GROUP_INNER = HEADS_PER_STATE_GROUP * SSM_HEAD_DIM
NEG_BIG = -1e30
LOG2E = math.log2(math.e)
VMEM_LIMIT = 56 * 1024 * 1024

TM_PROJ = 512
XBC_PIECES = 4
TQ_ATTN = 512
TC_CONV = 512
TS_SCAN = 1024
TM_OUT = 512
TM_FFN = 512
FFN_SUB = 512
FF_CHUNK = 256
ACT_SLOTS = 3
HALO = 8
HALO_BF16 = 16


def _dot(a, b):
    return jnp.dot(a, b, preferred_element_type=F32)


def _dot_nt(a, b):
    return lax.dot_general(a, b, (((1,), (1,)), ((), ())), preferred_element_type=F32)


def _dot_tn(a, b):
    return lax.dot_general(a, b, (((0,), (0,)), ((), ())), preferred_element_type=F32)


def _split3(a):
    hi = a.astype(BF16)
    r = a - hi.astype(F32)
    mid = r.astype(BF16)
    lo = (r - mid.astype(F32)).astype(BF16)
    return hi, mid, lo


def _silu(x):
    return x / (1.0 + jnp.exp(-x))


def _softplus(x):
    return jnp.maximum(x, 0.0) + jnp.log1p(jnp.exp(-jnp.abs(x)))


def _shift_rows(xe, off, n, delta):
    cur = xe[off:off + n]
    if delta == 0:
        return cur
    r = lax.broadcasted_iota(jnp.int32, cur.shape, 0) % HALO
    if delta < 0:
        merged = jnp.where(r < HALO + delta, cur, xe[off - HALO:off + n - HALO])
        rot = -delta
    else:
        merged = jnp.where(r >= delta, cur, xe[off + HALO:off + n + HALO])
        rot = HALO - delta
    width = cur.shape[1]
    return pltpu.roll(merged.reshape(n // HALO, HALO, width), rot, axis=1).reshape(n, width)


def _params(sem):
    return pltpu.CompilerParams(dimension_semantics=sem, vmem_limit_bytes=VMEM_LIMIT)


def _const_spec(shape):
    nd = len(shape)
    return pl.BlockSpec(shape, lambda *_: (0,) * nd)


def _resident_spec(shape):
    nd = len(shape)
    return pl.BlockSpec(shape, lambda *_: (0,) * nd, pipeline_mode=pl.Buffered(1))


def _in_proj_kernel(x_ref, xprev_ref, xnext_ref, n1_ref, wqk_ref, wv_ref, wz_ref, wx_ref, wdtt_ref, wmq_ref,
                    hsum_ref, hexp_ref, qkg_ref, mqg_ref, cos_ref, sin_ref, convw_ref, convb_ref,
                    q_ref, k_ref, v_ref, z_ref, xbc_ref, dtt_ref, mq_ref, xe_ref, *, blocks_per_seq):
    tm = x_ref.shape[0]
    n1 = n1_ref[...]

    def norm(x):
        ms = jnp.mean(x * x, axis=-1, keepdims=True)
        return (x * lax.rsqrt(ms + EPS) * n1).astype(BF16)

    h = norm(x_ref[...])

    pos = pl.program_id(0) % blocks_per_seq
    hb = HALO_BF16
    pad = SSM_CONV // 2
    hext = jnp.concatenate([norm(xprev_ref[...]), h, norm(xnext_ref[...])], axis=0)
    keep_prev = (pos > 0).astype(F32)
    keep_next = (pos < blocks_per_seq - 1).astype(F32)

    piece = XBC_WIDTH // XBC_PIECES

    def project(p):
        r = _dot(hext, wx_ref[:, p * piece:(p + 1) * piece])
        slot = p % 2
        xe_ref[slot, :hb, :] = r[:hb] * keep_prev
        xe_ref[slot, hb:hb + tm, :] = r[hb:hb + tm]
        xe_ref[slot, hb + tm:, :] = r[hb + tm:] * keep_next

    def conv(p):
        c0, c1, slot = p * piece, (p + 1) * piece, p % 2
        y = convb_ref[:, c0:c1]
        for t in range(SSM_CONV):
            y = y + xe_ref[slot, pl.ds(hb - pad + t, tm), :] * convw_ref[t:t + 1, c0:c1]
        xbc_ref[:, c0:c1] = _silu(y).astype(BF16)

    qk = _dot(h, wqk_ref[...])
    project(0)
    project(1)
    v_ref[...] = _dot(h, wv_ref[...]).astype(BF16)
    ssq = _dot((qk * qk).astype(BF16), hsum_ref[...])
    conv(0)
    project(2)
    z_ref[...] = _dot(h, wz_ref[...]).astype(BF16)
    inv = lax.rsqrt(ssq * (1.0 / HEAD_DIM) + EPS)
    inv_hi = inv.astype(BF16)
    inv_lo = (inv - inv_hi.astype(F32)).astype(BF16)
    scale = _dot(jnp.concatenate([inv_hi, inv_lo], axis=1), hexp_ref[...])
    conv(1)
    project(3)
    mq = _dot(h, wmq_ref[...])
    dtt_ref[...] = _dot_nt(wdtt_ref[...], h)
    conv(2)
    conv(3)
    mqg = mqg_ref[...] * (MEM_HEAD_DIM ** -0.5 * LOG2E)
    for hh in range(MEM_HEADS):
        m = mq[:, hh * LANES:(hh + 1) * LANES]
        r = lax.rsqrt(jnp.mean(m * m, axis=-1, keepdims=True) + EPS)
        mq_ref[:, hh * LANES:(hh + 1) * LANES] = (m * r * mqg).astype(BF16)

    qkn = qk * scale * qkg_ref[...]

    cos = cos_ref[...]
    sin = sin_ref[...]
    for p in range(GQA_GROUP):
        u = qkn[:, p * 256:p * 256 + LANES]
        w = qkn[:, p * 256 + LANES:(p + 1) * 256]
        q_ref[:, p * 256:p * 256 + LANES] = (u * cos - w * sin).astype(BF16)
        q_ref[:, p * 256 + LANES:(p + 1) * 256] = (w * cos + u * sin).astype(BF16)
    u = qkn[:, ATTN_WIDTH:ATTN_WIDTH + LANES]
    w = qkn[:, ATTN_WIDTH + LANES:QK_WIDTH]
    k_ref[:, :LANES] = (u * cos - w * sin).astype(BF16)
    k_ref[:, LANES:] = (w * cos + u * sin).astype(BF16)


def _in_proj(x2d, s, w):
    t = x2d.shape[0]
    tm = TM_PROJ
    nblk_seq = s // tm
    row = lambda i: (i, 0)
    tab = lambda i: (i % nblk_seq, 0)
    r = tm // HALO_BF16
    nh = t // HALO_BF16
    in_specs = [
        pl.BlockSpec((tm, D_MODEL), row),
        pl.BlockSpec((HALO_BF16, D_MODEL), lambda i: (jnp.maximum(i * r - 1, 0), 0)),
        pl.BlockSpec((HALO_BF16, D_MODEL), lambda i: (jnp.minimum((i + 1) * r, nh - 1), 0)),
        _const_spec((1, D_MODEL)),
        _const_spec((D_MODEL, QK_WIDTH)),
        _const_spec((D_MODEL, KV_WIDTH)),
        _const_spec((D_MODEL, SSM_INNER)),
        _const_spec((D_MODEL, XBC_WIDTH)),
        _const_spec((DT_WIDTH, D_MODEL)),
        _const_spec((D_MODEL, MEM_WIDTH)),
        _const_spec((QK_WIDTH, LANES)),
        _const_spec((2 * LANES, QK_WIDTH)),
        _const_spec((1, QK_WIDTH)),
        _const_spec((1, LANES)),
        pl.BlockSpec((tm, LANES), tab),
        pl.BlockSpec((tm, LANES), tab),
        _const_spec((SSM_CONV, XBC_WIDTH)),
        _const_spec((1, XBC_WIDTH)),
    ]
    out_shape = [
        jax.ShapeDtypeStruct((t, ATTN_WIDTH), BF16),
        jax.ShapeDtypeStruct((t, KV_WIDTH), BF16),
        jax.ShapeDtypeStruct((t, KV_WIDTH), BF16),
        jax.ShapeDtypeStruct((t, SSM_INNER), BF16),
        jax.ShapeDtypeStruct((t, XBC_WIDTH), BF16),
        jax.ShapeDtypeStruct((DT_WIDTH, t), F32),
        jax.ShapeDtypeStruct((t, MEM_WIDTH), BF16),
    ]
    out_specs = [
        pl.BlockSpec((tm, ATTN_WIDTH), row),
        pl.BlockSpec((tm, KV_WIDTH), row),
        pl.BlockSpec((tm, KV_WIDTH), row),
        pl.BlockSpec((tm, SSM_INNER), row),
        pl.BlockSpec((tm, XBC_WIDTH), row),
        pl.BlockSpec((DT_WIDTH, tm), lambda i: (0, i)),
        pl.BlockSpec((tm, MEM_WIDTH), row),
    ]
    return pl.pallas_call(
        functools.partial(_in_proj_kernel, blocks_per_seq=nblk_seq),
        grid=(t // tm,), in_specs=in_specs, out_specs=out_specs, out_shape=out_shape,
        scratch_shapes=[pltpu.VMEM((2, tm + 2 * HALO_BF16, XBC_WIDTH // XBC_PIECES), F32)],
        compiler_params=_params(("parallel",)), name="in_proj",
    )(x2d, x2d, x2d, w["n1"], w["wqk"], w["wv"], w["wz"], w["wx"], w["wdtt"], w["wmq"],
      w["hsum"], w["hexp"], w["qkg"], w["mqg"], w["cos"][:s], w["sin"][:s], w["convw"], w["convb"])


def _mem_kv_kernel(mem_ref, nw_ref, w_ref, mkg_ref, mk_ref, mv_ref):
    x = mem_ref[0]
    ms = jnp.mean(x * x, axis=-1, keepdims=True)
    h = (x * lax.rsqrt(ms + EPS) * nw_ref[...]).astype(BF16)
    kv = _dot(h, w_ref[...])
    g = mkg_ref[...]
    for hh in range(MEM_HEADS):
        m = kv[:, hh * LANES:(hh + 1) * LANES]
        r = lax.rsqrt(jnp.mean(m * m, axis=-1, keepdims=True) + EPS)
        mk_ref[0, :, hh * LANES:(hh + 1) * LANES] = (m * r * g).astype(BF16)
    mv_ref[0] = kv[:, MEM_WIDTH:].astype(BF16)


def _mem_kv(mem, w):
    b = mem.shape[0]
    blk = lambda i: (i, 0, 0)
    return pl.pallas_call(
        _mem_kv_kernel, grid=(b,),
        in_specs=[pl.BlockSpec((1, N_MEM, D_MODEL), blk), _const_spec((1, D_MODEL)),
                  _const_spec((D_MODEL, 2 * MEM_WIDTH)), _const_spec((1, LANES))],
        out_specs=[pl.BlockSpec((1, N_MEM, MEM_WIDTH), blk), pl.BlockSpec((1, N_MEM, MEM_WIDTH), blk)],
        out_shape=[jax.ShapeDtypeStruct((b, N_MEM, MEM_WIDTH), BF16)] * 2,
        compiler_params=_params(("parallel",)), name="mem_kv",
    )(mem, w["memn"], w["wmemkv"], w["mkg"])


def _attention_kernel(sink_ref, q_ref, k_ref, v_ref, mq_ref, mk_ref, mv_ref, o_ref, mo_ref, *, seq):
    qi = pl.program_id(1)
    tq = q_ref.shape[1]
    kwin = 3 * BLOCK

    lane_k = lax.broadcasted_iota(jnp.int32, (kwin, 2 * LANES), 1)
    lane_o = lax.broadcasted_iota(jnp.int32, (BLOCK, 2 * LANES), 1)
    row_i = lax.broadcasted_iota(jnp.int32, (BLOCK, kwin), 0)
    col_i = lax.broadcasted_iota(jnp.int32, (BLOCK, kwin), 1)

    windows = {}

    def window(blk):
        if blk not in windows:
            r0 = qi * tq + blk * BLOCK
            ks = pl.multiple_of(jnp.clip(r0 - BLOCK, 0, seq - kwin), BLOCK)
            kw = k_ref[0, pl.ds(ks, kwin), :]
            vw = v_ref[0, pl.ds(ks, kwin), :]
            delta = (col_i - row_i) + (ks - r0)
            bias = jnp.where(jnp.abs(delta) <= WINDOW, 0.0, NEG_BIG).astype(F32)
            zero = jnp.zeros_like(kw)
            kj = [jnp.where((lane_k % LANES) // HALF == j, kw, zero) for j in range(N_KV_HEADS)]
            kpairs = [jnp.concatenate(kj[2 * i:2 * i + 2], axis=0) for i in range(N_KV_HEADS // 2)]
            v4 = jnp.concatenate([jnp.where(lane_k // HEAD_DIM == j, vw, zero) for j in range(N_KV_HEADS)], axis=0)
            windows[blk] = (bias, kpairs, v4)
        return windows[blk]

    def logits(item):
        if item[0] == "w":
            _, blk, p = item
            _, kpairs, _ = window(blk)
            qp = q_ref[0, blk * BLOCK:(blk + 1) * BLOCK, p * 256:(p + 1) * 256]
            pairs = [_dot_nt(qp, kp) for kp in kpairs]
            return [pr[:, i * kwin:(i + 1) * kwin] for pr in pairs for i in range(2)]
        hh = item[1]
        return _dot_nt(mq_ref[0, :, hh * LANES:(hh + 1) * LANES], mk_ref[0, :, hh * LANES:(hh + 1) * LANES])

    def softmax(item, sc):
        if item[0] == "w":
            _, blk, p = item
            bias, _, _ = window(blk)
            probs, invs = [], []
            for j in range(N_KV_HEADS):
                snk = sink_ref[j * GQA_GROUP + p] * LOG2E
                s = sc[j] + bias
                m = jnp.maximum(jnp.max(s, axis=-1, keepdims=True), snk)
                e = jnp.exp2(s - m)
                den = jnp.sum(e, axis=-1, keepdims=True) + jnp.exp2(snk - m)
                probs.append(e.astype(BF16))
                invs.append(1.0 / den)
            inv = jnp.where(lane_o < HEAD_DIM, invs[0],
                            jnp.where(lane_o < 2 * HEAD_DIM, invs[1],
                                      jnp.where(lane_o < 3 * HEAD_DIM, invs[2], invs[3])))
            return jnp.concatenate(probs, axis=1), inv
        m = jnp.max(sc, axis=-1, keepdims=True)
        e = jnp.exp2(sc - m)
        return e.astype(BF16), 1.0 / jnp.sum(e, axis=-1, keepdims=True)

    def values(item, pr):
        probs, inv = pr
        if item[0] == "w":
            _, blk, p = item
            _, _, v4 = window(blk)
            o_ref[0, blk * BLOCK:(blk + 1) * BLOCK, p * 256:(p + 1) * 256] = (_dot(probs, v4) * inv).astype(BF16)
        else:
            hh = item[1]
            o = _dot(probs, mv_ref[0, :, hh * LANES:(hh + 1) * LANES])
            mo_ref[0, :, hh * LANES:(hh + 1) * LANES] = (o * inv).astype(BF16)

    items = [("w", blk, p) for blk in range(tq // BLOCK) for p in range(GQA_GROUP)]
    items += [("m", hh) for hh in range(MEM_HEADS)]
    n = len(items)
    sc_next = logits(items[0])
    pr_prev = None
    for i in range(n):
        sc_cur = sc_next
        if i + 1 < n:
            sc_next = logits(items[i + 1])
        if pr_prev is not None:
            values(items[i - 1], pr_prev)
        pr_prev = softmax(items[i], sc_cur)
    values(items[n - 1], pr_prev)


def _attention(q, k, v, mq, mk, mv, sink):
    b, s, _ = q.shape
    tq = TQ_ATTN
    qblk = lambda bi, i: (bi, i, 0)
    full = lambda bi, i: (bi, 0, 0)
    return pl.pallas_call(
        functools.partial(_attention_kernel, seq=s), grid=(b, s // tq),
        in_specs=[pl.BlockSpec(memory_space=pltpu.SMEM),
                  pl.BlockSpec((1, tq, ATTN_WIDTH), qblk),
                  pl.BlockSpec((1, s, KV_WIDTH), full),
                  pl.BlockSpec((1, s, KV_WIDTH), full),
                  pl.BlockSpec((1, tq, MEM_WIDTH), qblk),
                  pl.BlockSpec((1, N_MEM, MEM_WIDTH), full),
                  pl.BlockSpec((1, N_MEM, MEM_WIDTH), full)],
        out_specs=[pl.BlockSpec((1, tq, ATTN_WIDTH), qblk), pl.BlockSpec((1, tq, MEM_WIDTH), qblk)],
        out_shape=[jax.ShapeDtypeStruct((b, s, ATTN_WIDTH), BF16), jax.ShapeDtypeStruct((b, s, MEM_WIDTH), BF16)],
        compiler_params=_params(("parallel", "arbitrary")), name="attention",
    )(sink, q, k, v, mq, mk, mv)


def _slab(rows, sel):
    hi = rows.astype(BF16)
    lo = (rows - hi.astype(F32)).astype(BF16)
    return _dot_tn(jnp.concatenate([hi, lo], axis=0), sel)


def _ssd_scan_kernel(xf_ref, xb_ref, dttf_ref, dttb_ref, bias_c_ref, alog_c_ref, dskip_ref,
                     onehot_ref, self_ref, selb_ref, ya_ref, yb_ref, hf_ref, hb_ref):
    c = pl.program_id(1)

    @pl.when(c == 0)
    def _():
        hf_ref[...] = jnp.zeros_like(hf_ref)
        hb_ref[...] = jnp.zeros_like(hb_ref)

    ln = SSM_CHUNK
    ri = lax.broadcasted_iota(jnp.int32, (ln, ln), 0)
    cj = lax.broadcasted_iota(jnp.int32, (ln, ln), 1)
    low_incl = (cj <= ri)
    tri_l = jnp.where(low_incl, 1.0, 0.0).astype(BF16)
    tri_u = jnp.where(cj >= ri, 1.0, 0.0).astype(BF16)

    a_col = -jnp.exp(alog_c_ref[...])

    def exact_right(a, m):
        hi, mid, lo = _split3(a)
        return _dot(hi, m) + _dot(mid, m) + _dot(lo, m)

    def carry(x_chunk, slab_e, slab_w, edge, h_ref):
        x32 = x_chunk[:, :SSM_INNER].astype(F32)
        outs = []
        for g in range(SSM_GROUPS):
            lo_, hi_ = g * GROUP_INNER, (g + 1) * GROUP_INNER
            bm = x_chunk[:, SSM_INNER + g * SSM_STATE:SSM_INNER + (g + 1) * SSM_STATE]
            cm = x_chunk[:, SSM_INNER + SSM_GROUPS * SSM_STATE + g * SSM_STATE:SSM_INNER + SSM_GROUPS * SSM_STATE + (g + 1) * SSM_STATE]
            hprev = h_ref[g]
            outs.append(_dot(cm, hprev.astype(BF16)) * slab_e[:, lo_:hi_])
            xw = (x32[:, lo_:hi_] * slab_w[:, lo_:hi_]).astype(BF16)
            h_ref[g] = hprev * slab_e[edge:edge + 1, lo_:hi_] + _dot_tn(bm, xw)
        return jnp.concatenate(outs, axis=1)

    lt = cj < ri
    gt = cj > ri
    row16 = lax.broadcasted_iota(jnp.int32, (DT_WIDTH, ln), 0)
    trow = lax.broadcasted_iota(jnp.int32, (DT_WIDTH, 2 * ln), 0)
    lane = lax.broadcasted_iota(jnp.int32, (ln, LANES), 1)

    n_sub = xf_ref.shape[1] // ln

    def b_m(x_chunk, g):
        return x_chunk[:, SSM_INNER + g * SSM_STATE:SSM_INNER + (g + 1) * SSM_STATE]

    def c_m(x_chunk, g):
        o = SSM_INNER + SSM_GROUPS * SSM_STATE
        return x_chunk[:, o + g * SSM_STATE:o + (g + 1) * SSM_STATE]

    def prepare(k):
        kb = n_sub - 1 - k
        xc = xf_ref[0, k * ln:(k + 1) * ln, :]
        dt_r = _softplus(dttf_ref[:, k * ln:(k + 1) * ln] + bias_c_ref[...])
        a_r = dt_r * a_col
        cs_r = exact_right(a_r, tri_u)
        sf_r = exact_right(a_r, tri_l)
        dt_b = _softplus(dttb_ref[:, kb * ln:(kb + 1) * ln] + bias_c_ref[...])
        sf_b = exact_right(dt_b * a_col, tri_l)
        last = cs_r[:, ln - 1:ln]
        head = sf_b[:, 0:1]

        x_hi, x_mid, x_lo = _split3(jnp.where(row16 < SSM_HEADS, cs_r, sf_r))
        p_mat = jnp.concatenate([x_hi, x_mid, x_lo, jnp.ones((DT_WIDTH, ln), BF16)], axis=0)
        x_terms = [t.astype(F32) for t in (x_hi, x_mid, x_lo)]
        dds = []
        for hd in range(SSM_HEADS):
            t = jnp.zeros((DT_WIDTH, 2 * ln), F32)
            for term, x in enumerate(x_terms):
                piece = jnp.concatenate([x[hd:hd + 1, :], x[SSM_HEADS + hd:SSM_HEADS + hd + 1, :]], axis=1)
                t = jnp.where(trow == term, -piece, t)
            q_mat = jnp.concatenate([onehot_ref[hd], t.astype(BF16)], axis=0)
            dds.append(_dot_tn(p_mat, q_mat))
        return dict(
            xc=xc, dt=dt_r, dds=dds,
            cb=[_dot_nt(c_m(xc, g), b_m(xc, g)) for g in range(SSM_GROUPS)],
            f_e=_slab(jnp.exp(cs_r), self_ref[...]), f_w=_slab(dt_r * jnp.exp(last - cs_r), self_ref[...]),
            b_e=_slab(jnp.exp(sf_b), selb_ref[...]), b_w=_slab(dt_b * jnp.exp(head - sf_b), selb_ref[...]))

    def finish(k, p):
        kb = n_sub - 1 - k
        xc, dt_r = p["xc"], p["dt"]
        xs = xc[:, :SSM_INNER]
        y_parts = []
        for g in range(SSM_GROUPS):
            for pair in range(HEADS_PER_STATE_GROUP // 2):
                ms = []
                for e in range(2):
                    hd = g * HEADS_PER_STATE_GROUP + pair * 2 + e
                    hb_i = SSM_HEADS + hd
                    dd = p["dds"][hd]
                    ex = jnp.exp(jnp.where(low_incl, dd[:, :ln], dd[:, ln:]))
                    dtf_j = dt_r[hd:hd + 1, :]
                    dtb_j = dt_r[hb_i:hb_i + 1, :]
                    dts = jnp.where(lt, dtf_j, jnp.where(gt, dtb_j, dtf_j + dtb_j))
                    ms.append((p["cb"][g] * ex * dts).astype(BF16))
                col0 = (g * HEADS_PER_STATE_GROUP + pair * 2) * SSM_HEAD_DIM
                xp = xs[:, col0:col0 + LANES]
                zero = jnp.zeros_like(xp)
                xbd = jnp.concatenate([jnp.where(lane < SSM_HEAD_DIM, xp, zero),
                                       jnp.where(lane >= SSM_HEAD_DIM, xp, zero)], axis=0)
                y_parts.append(_dot(jnp.concatenate(ms, axis=1), xbd))
        y = jnp.concatenate(y_parts, axis=1) + xs.astype(F32) * dskip_ref[...]
        ya_ref[0, k * ln:(k + 1) * ln, :] = (y + carry(xc, p["f_e"], p["f_w"], ln - 1, hf_ref)).astype(BF16)
        yb_ref[0, kb * ln:(kb + 1) * ln, :] = carry(xb_ref[0, kb * ln:(kb + 1) * ln, :], p["b_e"], p["b_w"], 0,
                                                    hb_ref).astype(BF16)

    nxt = prepare(0)
    for k in range(n_sub):
        cur = nxt
        if k + 1 < n_sub:
            nxt = prepare(k + 1)
        finish(k, cur)


def _ssd_scan(xc, dtt, w):
    b, s, _ = xc.shape
    ts = TS_SCAN
    nc = s // ts
    fwd = lambda bi, c: (bi, c, 0)
    bwd = lambda bi, c: (bi, nc - 1 - c, 0)
    return pl.pallas_call(
        _ssd_scan_kernel, grid=(b, nc),
        in_specs=[pl.BlockSpec((1, ts, XBC_WIDTH), fwd),
                  pl.BlockSpec((1, ts, XBC_WIDTH), bwd),
                  pl.BlockSpec((DT_WIDTH, ts), lambda bi, c: (0, bi * nc + c)),
                  pl.BlockSpec((DT_WIDTH, ts), lambda bi, c: (0, bi * nc + nc - 1 - c)),
                  _const_spec((DT_WIDTH, 1)), _const_spec((DT_WIDTH, 1)),
                  _const_spec((1, SSM_INNER)),
                  _const_spec((SSM_HEADS, 3 * DT_WIDTH, 2 * SSM_CHUNK)),
                  _const_spec((2 * DT_WIDTH, SSM_INNER)), _const_spec((2 * DT_WIDTH, SSM_INNER))],
        out_specs=[pl.BlockSpec((1, ts, SSM_INNER), fwd), pl.BlockSpec((1, ts, SSM_INNER), bwd)],
        out_shape=[jax.ShapeDtypeStruct((b, s, SSM_INNER), BF16)] * 2,
        scratch_shapes=[pltpu.VMEM((SSM_GROUPS, SSM_STATE, GROUP_INNER), F32)] * 2,
        compiler_params=_params(("parallel", "arbitrary")), name="ssd_scan",
    )(xc, xc, dtt, dtt, w["dtb_c"], w["alog_c"], w["dskip"], w["onehot"], w["sel_f"], w["sel_b"])


def _out_proj_kernel(x_ref, attn_ref, ya_ref, yb_ref, z_ref, mem_ref, sn_ref, wo_ref, o_ref):
    y = (ya_ref[...].astype(F32) + yb_ref[...].astype(F32)) * _silu(z_ref[...].astype(F32))
    ms = jnp.mean(y * y, axis=-1, keepdims=True)
    ssm = (y * lax.rsqrt(ms + EPS) * sn_ref[...]).astype(BF16)
    mix = jnp.concatenate([attn_ref[...], mem_ref[...], ssm], axis=1)
    o_ref[...] = x_ref[...] + _dot(mix, wo_ref[...])


def _out_proj(x2d, attn, ya, yb, z, memo, w):
    t = x2d.shape[0]
    tm = TM_OUT
    row = lambda i: (i, 0)
    return pl.pallas_call(
        _out_proj_kernel, grid=(t // tm,),
        in_specs=[pl.BlockSpec((tm, D_MODEL), row), pl.BlockSpec((tm, ATTN_WIDTH), row),
                  pl.BlockSpec((tm, SSM_INNER), row), pl.BlockSpec((tm, SSM_INNER), row),
                  pl.BlockSpec((tm, SSM_INNER), row), pl.BlockSpec((tm, MEM_WIDTH), row),
                  _const_spec((1, SSM_INNER)), _resident_spec((MIX_WIDTH, D_MODEL))],
        out_specs=pl.BlockSpec((tm, D_MODEL), row),
        out_shape=jax.ShapeDtypeStruct((t, D_MODEL), F32),
        compiler_params=_params(("parallel",)), name="out_proj",
    )(x2d, attn, ya, yb, z, memo, w["ssmn"], w["wo"])


def _ffn_kernel(x_ref, prev_ref, next_ref, n2_ref, wg_ref, wu_ref, wd_ref, cw_ref, cb_ref, o_ref,
                g_ref, u_ref, act_ref, acc_ref):
    i = pl.program_id(1)
    ni = pl.num_programs(1)
    tm = x_ref.shape[1]
    n2 = n2_ref[...]

    def norm(v):
        ms = jnp.mean(v * v, axis=-1, keepdims=True)
        return (v * lax.rsqrt(ms + EPS) * n2).astype(BF16)

    hl = HALO_BF16
    pad = FFN_CONV // 2
    ts = FFN_SUB
    n_sub = tm // ts
    n_chunks = D_FF // FF_CHUNK

    hs, keeps = [], []
    for sub in range(n_sub):
        lo = norm(prev_ref[0]) if sub == 0 else norm(x_ref[0, sub * ts - hl:sub * ts, :])
        hi = norm(next_ref[0]) if sub == n_sub - 1 else norm(x_ref[0, (sub + 1) * ts:(sub + 1) * ts + hl, :])
        hs.append(jnp.concatenate([lo, norm(x_ref[0, sub * ts:(sub + 1) * ts, :]), hi], axis=0))
        keeps.append(((i > 0).astype(F32) if sub == 0 else None,
                      (i < ni - 1).astype(F32) if sub == n_sub - 1 else None))

    items = [(c, sub) for c in range(n_chunks) for sub in range(n_sub)]

    def up(it):
        c, sub = items[it]
        c0 = c * FF_CHUNK
        slot = it % 2
        keep_prev, keep_next = keeps[sub]
        for dst, wref in ((g_ref, wg_ref), (u_ref, wu_ref)):
            r = _dot(hs[sub], wref[:, c0:c0 + FF_CHUNK])
            dst[slot, 0:hl, :] = r[0:hl, :] if keep_prev is None else r[0:hl, :] * keep_prev
            dst[slot, hl:hl + ts, :] = r[hl:hl + ts, :]
            dst[slot, hl + ts:, :] = r[hl + ts:, :] if keep_next is None else r[hl + ts:, :] * keep_next

    def conv_act(it):
        c, _ = items[it]
        c0 = c * FF_CHUNK
        slot = it % 2
        outs = []
        for src, off in ((g_ref, 0), (u_ref, D_FF)):
            cw = cw_ref[:, off + c0:off + c0 + FF_CHUNK]
            cb = cb_ref[:, off + c0:off + c0 + FF_CHUNK]
            y = cb + src[slot, pl.ds(hl - pad, ts), :] * cw[0:1, :]
            for t in range(1, FFN_CONV):
                y = y + src[slot, pl.ds(hl - pad + t, ts), :] * cw[t:t + 1, :]
            outs.append(y)
        act_ref[it % ACT_SLOTS] = (_silu(outs[0]) * outs[1]).astype(BF16)

    def down(it):
        c, sub = items[it]
        c0 = c * FF_CHUNK
        part = _dot(act_ref[it % ACT_SLOTS], wd_ref[c0:c0 + FF_CHUNK, :])
        if c == 0:
            acc_ref[sub] = part
        else:
            acc_ref[sub] += part

    lag = ACT_SLOTS - 1
    n_items = len(items)
    up(0)
    for it in range(n_items):
        if it + 1 < n_items:
            up(it + 1)
        if it >= lag:
            down(it - lag)
        conv_act(it)
    for it in range(n_items - lag, n_items):
        down(it)
    for sub in range(n_sub):
        o_ref[0, sub * ts:(sub + 1) * ts, :] = x_ref[0, sub * ts:(sub + 1) * ts, :] + acc_ref[sub]


def _ffn(x1, w):
    b, s, _ = x1.shape
    tm = TM_FFN
    hl = HALO_BF16
    r = tm // hl
    nh = s // hl
    return pl.pallas_call(
        _ffn_kernel, grid=(b, s // tm),
        in_specs=[pl.BlockSpec((1, tm, D_MODEL), lambda bi, i: (bi, i, 0)),
                  pl.BlockSpec((1, hl, D_MODEL), lambda bi, i: (bi, jnp.maximum(i * r - 1, 0), 0)),
                  pl.BlockSpec((1, hl, D_MODEL), lambda bi, i: (bi, jnp.minimum((i + 1) * r, nh - 1), 0)),
                  _const_spec((1, D_MODEL)),
                  _resident_spec((D_MODEL, D_FF)), _resident_spec((D_MODEL, D_FF)), _resident_spec((D_FF, D_MODEL)),
                  _const_spec((FFN_CONV, 2 * D_FF)), _const_spec((1, 2 * D_FF))],
        out_specs=pl.BlockSpec((1, tm, D_MODEL), lambda bi, i: (bi, i, 0)),
        out_shape=jax.ShapeDtypeStruct((b, s, D_MODEL), F32),
        scratch_shapes=[pltpu.VMEM((2, FFN_SUB + 2 * hl, FF_CHUNK), F32),
                        pltpu.VMEM((2, FFN_SUB + 2 * hl, FF_CHUNK), F32),
                        pltpu.VMEM((ACT_SLOTS, FFN_SUB, FF_CHUNK), BF16),
                        pltpu.VMEM((tm // FFN_SUB, FFN_SUB, D_MODEL), F32)],
        compiler_params=_params(("parallel", "parallel")), name="ffn",
    )(x1, x1, x1, w["n2"], w["wg"], w["wu"], w["wd"], w["fcw"], w["fcb"])


def _prep_weights(max_seq, norm1_w, w_in, q_norm_w, k_norm_w, attn_sink, ssm_conv_w, ssm_conv_b, ssm_dt_bias,
                  ssm_A_log, ssm_D, ssm_norm_w, mem_norm_w, w_mem_kv, mq_norm_w, mk_norm_w, w_out, norm2_w,
                  w_ffn_up, ffn_conv_w, ffn_conv_b, w_ffn_down):
    o = 0
    wq = w_in[:, o:o + ATTN_WIDTH]; o += ATTN_WIDTH
    wk = w_in[:, o:o + KV_WIDTH]; o += KV_WIDTH
    wv = w_in[:, o:o + KV_WIDTH]; o += KV_WIDTH
    wz = w_in[:, o:o + SSM_INNER]; o += SSM_INNER
    wx = w_in[:, o:o + XBC_WIDTH]; o += XBC_WIDTH
    wdt = w_in[:, o:o + DT_WIDTH]; o += DT_WIDTH
    wmq = w_in[:, o:o + MEM_WIDTH]

    wq_p = wq.reshape(D_MODEL, N_KV_HEADS, GQA_GROUP, 2, HALF).transpose(0, 2, 3, 1, 4).reshape(D_MODEL, ATTN_WIDTH)
    wk_p = wk.reshape(D_MODEL, N_KV_HEADS, 2, HALF).transpose(0, 2, 1, 3).reshape(D_MODEL, KV_WIDTH)
    qg = jnp.broadcast_to(q_norm_w.reshape(1, 2, 1, HALF), (GQA_GROUP, 2, N_KV_HEADS, HALF)).reshape(ATTN_WIDTH)
    kg = jnp.broadcast_to(k_norm_w.reshape(2, 1, HALF), (2, N_KV_HEADS, HALF)).reshape(KV_WIDTH)
    qkg = jnp.concatenate([qg * (HEAD_DIM ** -0.5 * LOG2E), kg]).reshape(1, QK_WIDTH)

    cq = np.arange(ATTN_WIDTH)
    eq = (cq // 256) * N_KV_HEADS + (cq % LANES) // HALF
    ck = np.arange(KV_WIDTH)
    ek = N_Q_HEADS + (ck % LANES) // HALF
    e_all = np.concatenate([eq, ek])
    hsum = np.zeros((QK_WIDTH, LANES), np.float32)
    hsum[np.arange(QK_WIDTH), e_all] = 1.0
    hexp = np.concatenate([hsum.T, hsum.T], axis=0)

    inv = ROPE_THETA ** (-jnp.arange(0, HEAD_DIM, 2, dtype=F32) / HEAD_DIM)
    ang = jnp.arange(max_seq, dtype=F32)[:, None] * inv[None, :]
    cos = jnp.tile(jnp.cos(ang), (1, LANES // HALF))
    sin = jnp.tile(jnp.sin(ang), (1, LANES // HALF))

    wo_a = w_out[:ATTN_WIDTH].reshape(N_KV_HEADS, GQA_GROUP, HEAD_DIM, D_MODEL).transpose(1, 0, 2, 3)
    wo_a = wo_a.reshape(ATTN_WIDTH, D_MODEL)

    r48 = np.arange(3 * DT_WIDTH) % DT_WIDTH
    onehot = np.zeros((SSM_HEADS, 3 * DT_WIDTH, 2 * SSM_CHUNK), np.float32)
    for hd in range(SSM_HEADS):
        onehot[hd, r48 == hd, :SSM_CHUNK] = 1.0
        onehot[hd, r48 == SSM_HEADS + hd, SSM_CHUNK:] = 1.0
    r32 = np.arange(2 * DT_WIDTH) % DT_WIDTH
    col_head = np.arange(SSM_INNER) // SSM_HEAD_DIM
    sel_f = (r32[:, None] == col_head[None, :]).astype(np.float32)
    sel_b = (r32[:, None] == SSM_HEADS + col_head[None, :]).astype(np.float32)
    return {
        "onehot": jnp.asarray(onehot, BF16), "sel_f": jnp.asarray(sel_f, BF16), "sel_b": jnp.asarray(sel_b, BF16),
        "n1": norm1_w.reshape(1, D_MODEL),
        "wqk": jnp.concatenate([wq_p, wk_p], axis=1).astype(BF16),
        "wv": wv.astype(BF16), "wz": wz.astype(BF16), "wx": wx.astype(BF16),
        "wdtt": wdt.T.astype(BF16), "wmq": wmq.astype(BF16),
        "hsum": jnp.asarray(hsum, BF16), "hexp": jnp.asarray(hexp, BF16),
        "qkg": qkg, "mqg": mq_norm_w.reshape(1, LANES), "cos": cos, "sin": sin,
        "sink": attn_sink,
        "memn": mem_norm_w.reshape(1, D_MODEL), "wmemkv": w_mem_kv.astype(BF16), "mkg": mk_norm_w.reshape(1, LANES),
        "convw": ssm_conv_w, "convb": ssm_conv_b.reshape(1, XBC_WIDTH),
        "dtb_c": ssm_dt_bias.reshape(DT_WIDTH, 1), "alog_c": ssm_A_log.reshape(DT_WIDTH, 1),
        "dskip": jnp.repeat(ssm_D, SSM_HEAD_DIM).reshape(1, SSM_INNER),
        "ssmn": ssm_norm_w.reshape(1, SSM_INNER),
        "wo": jnp.concatenate([wo_a, w_out[ATTN_WIDTH + SSM_INNER:],
                               w_out[ATTN_WIDTH:ATTN_WIDTH + SSM_INNER]], axis=0).astype(BF16),
        "n2": norm2_w.reshape(1, D_MODEL),
        "wg": w_ffn_up[:, :D_FF].astype(BF16), "wu": w_ffn_up[:, D_FF:].astype(BF16),
        "wd": w_ffn_down.astype(BF16),
        "fcw": ffn_conv_w, "fcb": ffn_conv_b.reshape(1, 2 * D_FF),
    }


def _encoder_layer(x, mem, w):
    b, s, _ = x.shape
    t = b * s
    x2d = x.reshape(t, D_MODEL)
    q, k, v, z, xbc, dtt, mq = _in_proj(x2d, s, w)
    mk, mv = _mem_kv(mem, w)
    attn, memo = _attention(q.reshape(b, s, -1), k.reshape(b, s, -1), v.reshape(b, s, -1),
                            mq.reshape(b, s, -1), mk, mv, w["sink"])
    ya, yb = _ssd_scan(xbc.reshape(b, s, -1), dtt, w)
    x1 = _out_proj(x2d, attn.reshape(t, -1), ya.reshape(t, -1), yb.reshape(t, -1), z, memo.reshape(t, -1), w)
    return _ffn(x1.reshape(b, s, D_MODEL), w)


def kernel(x_prompt, x_sample, mem_prompt, mem_sample, norm1_w, w_in, q_norm_w, k_norm_w, attn_sink, ssm_conv_w, ssm_conv_b, ssm_dt_bias, ssm_A_log, ssm_D, ssm_norm_w, mem_norm_w, w_mem_kv, mq_norm_w, mk_norm_w, w_out, norm2_w, w_ffn_up, ffn_conv_w, ffn_conv_b, w_ffn_down):
    weights = (norm1_w, w_in, q_norm_w, k_norm_w, attn_sink, ssm_conv_w, ssm_conv_b, ssm_dt_bias,
               ssm_A_log, ssm_D, ssm_norm_w, mem_norm_w, w_mem_kv, mq_norm_w, mk_norm_w, w_out,
               norm2_w, w_ffn_up, ffn_conv_w, ffn_conv_b, w_ffn_down)
    depth = norm1_w.shape[0]
    max_seq = max(x_prompt.shape[1], x_sample.shape[1])
    y_prompt, y_sample = x_prompt, x_sample
    for layer in range(depth):
        w = _prep_weights(max_seq, *[p[layer] for p in weights])
        y_prompt = _encoder_layer(y_prompt, mem_prompt, w)
        y_sample = _encoder_layer(y_sample, mem_sample, w)
    return (y_prompt, y_sample)
```

```python
import functools
import math

import numpy as np
import jax
import jax.numpy as jnp
from jax import lax
from jax.experimental import pallas as pl
from jax.experimental.pallas import tpu as pltpu

F32 = jnp.float32
BF16 = jnp.bfloat16

D_MODEL = 1024
HEAD_DIM = 64
HALF = HEAD_DIM // 2
N_Q_HEADS = 16
N_KV_HEADS = 4
GQA_GROUP = N_Q_HEADS // N_KV_HEADS
ATTN_WIDTH = N_Q_HEADS * HEAD_DIM
KV_WIDTH = N_KV_HEADS * HEAD_DIM
WINDOW = 128
BLOCK = 128
ROPE_THETA = 10000.0
SSM_HEADS = 8
SSM_HEAD_DIM = 64
SSM_INNER = SSM_HEADS * SSM_HEAD_DIM
SSM_GROUPS = 2
SSM_STATE = 128
SSM_CONV = 5
SSM_CHUNK = 128
XBC_WIDTH = SSM_INNER + 2 * SSM_GROUPS * SSM_STATE
DT_WIDTH = 2 * SSM_HEADS
N_MEM = 256
MEM_HEADS = 4
MEM_HEAD_DIM = 128
MEM_WIDTH = MEM_HEADS * MEM_HEAD_DIM
MIX_WIDTH = ATTN_WIDTH + SSM_INNER + MEM_WIDTH
D_FF = 2816
FFN_CONV = 3
EPS = 1e-6

LANES = 128
QK_WIDTH = ATTN_WIDTH + KV_WIDTH
HEADS_PER_STATE_GROUP = SSM_HEADS // SSM_GROUPS
GROUP_INNER = HEADS_PER_STATE_GROUP * SSM_HEAD_DIM
NEG_BIG = -1e30
LOG2E = math.log2(math.e)
VMEM_LIMIT = 56 * 1024 * 1024

TM_PROJ = 512
XBC_PIECES = 4
TQ_ATTN = 1024
TC_CONV = 512
SCAN_AHEAD = 1
TS_SCAN = 1024
TM_OUT = 1024
OUT_SUBTILES = 2
TM_FFN = 512
FFN_SUB = 512
FF_CHUNK = 256
ACT_SLOTS = 3
HALO = 8
HALO_BF16 = 16


def _dot(a, b):
    return jnp.dot(a, b, preferred_element_type=F32)


def _dot_nt(a, b):
    return lax.dot_general(a, b, (((1,), (1,)), ((), ())), preferred_element_type=F32)


def _dot_tn(a, b):
    return lax.dot_general(a, b, (((0,), (0,)), ((), ())), preferred_element_type=F32)


def _split3(a):
    hi = a.astype(BF16)
    r = a - hi.astype(F32)
    mid = r.astype(BF16)
    lo = (r - mid.astype(F32)).astype(BF16)
    return hi, mid, lo


def _silu(x):
    return x / (1.0 + jnp.exp(-x))


def _softplus(x):
    return jnp.maximum(x, 0.0) + jnp.log1p(jnp.exp(-jnp.abs(x)))


def _shift_rows(xe, off, n, delta):
    cur = xe[off:off + n]
    if delta == 0:
        return cur
    r = lax.broadcasted_iota(jnp.int32, cur.shape, 0) % HALO
    if delta < 0:
        merged = jnp.where(r < HALO + delta, cur, xe[off - HALO:off + n - HALO])
        rot = -delta
    else:
        merged = jnp.where(r >= delta, cur, xe[off + HALO:off + n + HALO])
        rot = HALO - delta
    width = cur.shape[1]
    return pltpu.roll(merged.reshape(n // HALO, HALO, width), rot, axis=1).reshape(n, width)


def _params(sem):
    return pltpu.CompilerParams(dimension_semantics=sem, vmem_limit_bytes=VMEM_LIMIT)


def _const_spec(shape):
    nd = len(shape)
    return pl.BlockSpec(shape, lambda *_: (0,) * nd)


def _resident_spec(shape):
    nd = len(shape)
    return pl.BlockSpec(shape, lambda *_: (0,) * nd, pipeline_mode=pl.Buffered(1))


def _in_proj_kernel(x_ref, xprev_ref, xnext_ref, n1_ref, wqk_ref, wv_ref, wz_ref, wx_ref, wdtt_ref, wmq_ref,
                    hsum_ref, hexp_ref, qkg_ref, mqg_ref, cos_ref, sin_ref, convw_ref, convb_ref,
                    q_ref, k_ref, v_ref, z_ref, xbc_ref, dtt_ref, mq_ref, xe_ref, *, blocks_per_seq):
    tm = x_ref.shape[0]
    n1 = n1_ref[...]

    def norm(x):
        ms = jnp.mean(x * x, axis=-1, keepdims=True)
        return (x * lax.rsqrt(ms + EPS) * n1).astype(BF16)

    h = norm(x_ref[...])

    pos = pl.program_id(0) % blocks_per_seq
    hb = HALO_BF16
    pad = SSM_CONV // 2
    hext = jnp.concatenate([norm(xprev_ref[...]), h, norm(xnext_ref[...])], axis=0)
    keep_prev = (pos > 0).astype(F32)
    keep_next = (pos < blocks_per_seq - 1).astype(F32)

    piece = XBC_WIDTH // XBC_PIECES

    def project(p):
        r = _dot(hext, wx_ref[:, p * piece:(p + 1) * piece])
        slot = p % 2
        xe_ref[slot, :hb, :] = r[:hb] * keep_prev
        xe_ref[slot, hb:hb + tm, :] = r[hb:hb + tm]
        xe_ref[slot, hb + tm:, :] = r[hb + tm:] * keep_next

    def conv(p):
        c0, c1, slot = p * piece, (p + 1) * piece, p % 2
        y = convb_ref[:, c0:c1]
        for t in range(SSM_CONV):
            y = y + xe_ref[slot, pl.ds(hb - pad + t, tm), :] * convw_ref[t:t + 1, c0:c1]
        xbc_ref[:, c0:c1] = _silu(y).astype(BF16)

    qk = _dot(h, wqk_ref[...])
    project(0)
    project(1)
    v_ref[...] = _dot(h, wv_ref[...]).astype(BF16)
    ssq = _dot((qk * qk).astype(BF16), hsum_ref[...])
    conv(0)
    project(2)
    z_ref[...] = _dot(h, wz_ref[...]).astype(BF16)
    inv = lax.rsqrt(ssq * (1.0 / HEAD_DIM) + EPS)
    inv_hi = inv.astype(BF16)
    inv_lo = (inv - inv_hi.astype(F32)).astype(BF16)
    scale = _dot(jnp.concatenate([inv_hi, inv_lo], axis=1), hexp_ref[...])
    conv(1)
    project(3)
    mq = _dot(h, wmq_ref[...])
    dtt_ref[...] = _dot_nt(wdtt_ref[...], h)
    conv(2)
    conv(3)
    mqg = mqg_ref[...] * (MEM_HEAD_DIM ** -0.5 * LOG2E)
    for hh in range(MEM_HEADS):
        m = mq[:, hh * LANES:(hh + 1) * LANES]
        r = lax.rsqrt(jnp.mean(m * m, axis=-1, keepdims=True) + EPS)
        mq_ref[:, hh * LANES:(hh + 1) * LANES] = (m * r * mqg).astype(BF16)

    qkn = qk * scale * qkg_ref[...]

    cos = cos_ref[...]
    sin = sin_ref[...]
    for p in range(GQA_GROUP):
        u = qkn[:, p * 256:p * 256 + LANES]
        w = qkn[:, p * 256 + LANES:(p + 1) * 256]
        q_ref[:, p * 256:p * 256 + LANES] = (u * cos - w * sin).astype(BF16)
        q_ref[:, p * 256 + LANES:(p + 1) * 256] = (w * cos + u * sin).astype(BF16)
    u = qkn[:, ATTN_WIDTH:ATTN_WIDTH + LANES]
    w = qkn[:, ATTN_WIDTH + LANES:QK_WIDTH]
    k_ref[:, :LANES] = (u * cos - w * sin).astype(BF16)
    k_ref[:, LANES:] = (w * cos + u * sin).astype(BF16)


def _in_proj(x2d, s, w):
    t = x2d.shape[0]
    tm = TM_PROJ
    nblk_seq = s // tm
    row = lambda i: (i, 0)
    tab = lambda i: (i % nblk_seq, 0)
    r = tm // HALO_BF16
    nh = t // HALO_BF16
    in_specs = [
        pl.BlockSpec((tm, D_MODEL), row),
        pl.BlockSpec((HALO_BF16, D_MODEL), lambda i: (jnp.maximum(i * r - 1, 0), 0)),
        pl.BlockSpec((HALO_BF16, D_MODEL), lambda i: (jnp.minimum((i + 1) * r, nh - 1), 0)),
        _const_spec((1, D_MODEL)),
        _const_spec((D_MODEL, QK_WIDTH)),
        _const_spec((D_MODEL, KV_WIDTH)),
        _const_spec((D_MODEL, SSM_INNER)),
        _const_spec((D_MODEL, XBC_WIDTH)),
        _const_spec((DT_WIDTH, D_MODEL)),
        _const_spec((D_MODEL, MEM_WIDTH)),
        _const_spec((QK_WIDTH, LANES)),
        _const_spec((2 * LANES, QK_WIDTH)),
        _const_spec((1, QK_WIDTH)),
        _const_spec((1, LANES)),
        pl.BlockSpec((tm, LANES), tab),
        pl.BlockSpec((tm, LANES), tab),
        _const_spec((SSM_CONV, XBC_WIDTH)),
        _const_spec((1, XBC_WIDTH)),
    ]
    out_shape = [
        jax.ShapeDtypeStruct((t, ATTN_WIDTH), BF16),
        jax.ShapeDtypeStruct((t, KV_WIDTH), BF16),
        jax.ShapeDtypeStruct((t, KV_WIDTH), BF16),
        jax.ShapeDtypeStruct((t, SSM_INNER), BF16),
        jax.ShapeDtypeStruct((t, XBC_WIDTH), BF16),
        jax.ShapeDtypeStruct((DT_WIDTH, t), F32),
        jax.ShapeDtypeStruct((t, MEM_WIDTH), BF16),
    ]
    out_specs = [
        pl.BlockSpec((tm, ATTN_WIDTH), row),
        pl.BlockSpec((tm, KV_WIDTH), row),
        pl.BlockSpec((tm, KV_WIDTH), row),
        pl.BlockSpec((tm, SSM_INNER), row),
        pl.BlockSpec((tm, XBC_WIDTH), row),
        pl.BlockSpec((DT_WIDTH, tm), lambda i: (0, i)),
        pl.BlockSpec((tm, MEM_WIDTH), row),
    ]
    return pl.pallas_call(
        functools.partial(_in_proj_kernel, blocks_per_seq=nblk_seq),
        grid=(t // tm,), in_specs=in_specs, out_specs=out_specs, out_shape=out_shape,
        scratch_shapes=[pltpu.VMEM((2, tm + 2 * HALO_BF16, XBC_WIDTH // XBC_PIECES), F32)],
        compiler_params=_params(("parallel",)), name="in_proj",
    )(x2d, x2d, x2d, w["n1"], w["wqk"], w["wv"], w["wz"], w["wx"], w["wdtt"], w["wmq"],
      w["hsum"], w["hexp"], w["qkg"], w["mqg"], w["cos"][:s], w["sin"][:s], w["convw"], w["convb"])


def _mem_kv_kernel(mem_ref, nw_ref, w_ref, mkg_ref, mk_ref, mv_ref):
    x = mem_ref[0]
    ms = jnp.mean(x * x, axis=-1, keepdims=True)
    h = (x * lax.rsqrt(ms + EPS) * nw_ref[...]).astype(BF16)
    kv = _dot(h, w_ref[...])
    g = mkg_ref[...]
    for hh in range(MEM_HEADS):
        m = kv[:, hh * LANES:(hh + 1) * LANES]
        r = lax.rsqrt(jnp.mean(m * m, axis=-1, keepdims=True) + EPS)
        mk_ref[0, :, hh * LANES:(hh + 1) * LANES] = (m * r * g).astype(BF16)
    mv_ref[0] = kv[:, MEM_WIDTH:].astype(BF16)


def _mem_kv(mem, w):
    b = mem.shape[0]
    blk = lambda i: (i, 0, 0)
    return pl.pallas_call(
        _mem_kv_kernel, grid=(b,),
        in_specs=[pl.BlockSpec((1, N_MEM, D_MODEL), blk), _const_spec((1, D_MODEL)),
                  _const_spec((D_MODEL, 2 * MEM_WIDTH)), _const_spec((1, LANES))],
        out_specs=[pl.BlockSpec((1, N_MEM, MEM_WIDTH), blk), pl.BlockSpec((1, N_MEM, MEM_WIDTH), blk)],
        out_shape=[jax.ShapeDtypeStruct((b, N_MEM, MEM_WIDTH), BF16)] * 2,
        compiler_params=_params(("parallel",)), name="mem_kv",
    )(mem, w["memn"], w["wmemkv"], w["mkg"])


def _attention_kernel(sink_ref, q_ref, k_ref, v_ref, mq_ref, mk_ref, mv_ref, o_ref, mo_ref, *, seq):
    qi = pl.program_id(1)
    tq = q_ref.shape[1]
    kwin = 3 * BLOCK

    lane_k = lax.broadcasted_iota(jnp.int32, (kwin, 2 * LANES), 1)
    lane_o = lax.broadcasted_iota(jnp.int32, (BLOCK, 2 * LANES), 1)
    row_i = lax.broadcasted_iota(jnp.int32, (BLOCK, kwin), 0)
    col_i = lax.broadcasted_iota(jnp.int32, (BLOCK, kwin), 1)

    windows = {}

    def window(blk):
        if blk not in windows:
            r0 = qi * tq + blk * BLOCK
            ks = pl.multiple_of(jnp.clip(r0 - BLOCK, 0, seq - kwin), BLOCK)
            kw = k_ref[0, pl.ds(ks, kwin), :]
            vw = v_ref[0, pl.ds(ks, kwin), :]
            delta = (col_i - row_i) + (ks - r0)
            bias = jnp.where(jnp.abs(delta) <= WINDOW, 0.0, NEG_BIG).astype(F32)
            zero = jnp.zeros_like(kw)
            kj = [jnp.where((lane_k % LANES) // HALF == j, kw, zero) for j in range(N_KV_HEADS)]
            kpairs = [jnp.concatenate(kj[2 * i:2 * i + 2], axis=0) for i in range(N_KV_HEADS // 2)]
            v4 = jnp.concatenate([jnp.where(lane_k // HEAD_DIM == j, vw, zero) for j in range(N_KV_HEADS)], axis=0)
            windows[blk] = (bias, kpairs, v4)
        return windows[blk]

    def logits(item):
        if item[0] == "w":
            _, blk, p = item
            _, kpairs, _ = window(blk)
            qp = q_ref[0, blk * BLOCK:(blk + 1) * BLOCK, p * 256:(p + 1) * 256]
            pairs = [_dot_nt(qp, kp) for kp in kpairs]
            return [pr[:, i * kwin:(i + 1) * kwin] for pr in pairs for i in range(2)]
        hh = item[1]
        return _dot_nt(mq_ref[0, :, hh * LANES:(hh + 1) * LANES], mk_ref[0, :, hh * LANES:(hh + 1) * LANES])

    def softmax(item, sc):
        if item[0] == "w":
            _, blk, p = item
            bias, _, _ = window(blk)
            probs, invs = [], []
            for j in range(N_KV_HEADS):
                snk = sink_ref[j * GQA_GROUP + p] * LOG2E
                s = sc[j] + bias
                m = jnp.maximum(jnp.max(s, axis=-1, keepdims=True), snk)
                e = jnp.exp2(s - m)
                den = jnp.sum(e, axis=-1, keepdims=True) + jnp.exp2(snk - m)
                probs.append(e.astype(BF16))
                invs.append(1.0 / den)
            inv = jnp.where(lane_o < HEAD_DIM, invs[0],
                            jnp.where(lane_o < 2 * HEAD_DIM, invs[1],
                                      jnp.where(lane_o < 3 * HEAD_DIM, invs[2], invs[3])))
            return jnp.concatenate(probs, axis=1), inv
        m = jnp.max(sc, axis=-1, keepdims=True)
        e = jnp.exp2(sc - m)
        return e.astype(BF16), 1.0 / jnp.sum(e, axis=-1, keepdims=True)

    def values(item, pr):
        probs, inv = pr
        if item[0] == "w":
            _, blk, p = item
            _, _, v4 = window(blk)
            o_ref[0, blk * BLOCK:(blk + 1) * BLOCK, p * 256:(p + 1) * 256] = (_dot(probs, v4) * inv).astype(BF16)
        else:
            hh = item[1]
            o = _dot(probs, mv_ref[0, :, hh * LANES:(hh + 1) * LANES])
            mo_ref[0, :, hh * LANES:(hh + 1) * LANES] = (o * inv).astype(BF16)

    items = [("w", blk, p) for blk in range(tq // BLOCK) for p in range(GQA_GROUP)]
    items += [("m", hh) for hh in range(MEM_HEADS)]
    n = len(items)
    sc_next = logits(items[0])
    pr_prev = None
    for i in range(n):
        sc_cur = sc_next
        if i + 1 < n:
            sc_next = logits(items[i + 1])
        if pr_prev is not None:
            values(items[i - 1], pr_prev)
        pr_prev = softmax(items[i], sc_cur)
    values(items[n - 1], pr_prev)


def _attention(q, k, v, mq, mk, mv, sink):
    b, s, _ = q.shape
    tq = TQ_ATTN
    qblk = lambda bi, i: (bi, i, 0)
    full = lambda bi, i: (bi, 0, 0)
    return pl.pallas_call(
        functools.partial(_attention_kernel, seq=s), grid=(b, s // tq),
        in_specs=[pl.BlockSpec(memory_space=pltpu.SMEM),
                  pl.BlockSpec((1, tq, ATTN_WIDTH), qblk),
                  pl.BlockSpec((1, s, KV_WIDTH), full),
                  pl.BlockSpec((1, s, KV_WIDTH), full),
                  pl.BlockSpec((1, tq, MEM_WIDTH), qblk),
                  pl.BlockSpec((1, N_MEM, MEM_WIDTH), full),
                  pl.BlockSpec((1, N_MEM, MEM_WIDTH), full)],
        out_specs=[pl.BlockSpec((1, tq, ATTN_WIDTH), qblk), pl.BlockSpec((1, tq, MEM_WIDTH), qblk)],
        out_shape=[jax.ShapeDtypeStruct((b, s, ATTN_WIDTH), BF16), jax.ShapeDtypeStruct((b, s, MEM_WIDTH), BF16)],
        compiler_params=_params(("parallel", "arbitrary")), name="attention",
    )(sink, q, k, v, mq, mk, mv)


def _slab(rows, sel):
    hi = rows.astype(BF16)
    lo = (rows - hi.astype(F32)).astype(BF16)
    return _dot_tn(jnp.concatenate([hi, lo], axis=0), sel)


def _ssd_scan_kernel(xf_ref, xb_ref, dttf_ref, dttb_ref, bias_c_ref, alog_c_ref, dskip_ref,
                     onehot_ref, self_ref, selb_ref, ya_ref, yb_ref, hf_ref, hb_ref):
    c = pl.program_id(1)

    @pl.when(c == 0)
    def _():
        hf_ref[...] = jnp.zeros_like(hf_ref)
        hb_ref[...] = jnp.zeros_like(hb_ref)

    ln = SSM_CHUNK
    ri = lax.broadcasted_iota(jnp.int32, (ln, ln), 0)
    cj = lax.broadcasted_iota(jnp.int32, (ln, ln), 1)
    low_incl = (cj <= ri)
    tri_l = jnp.where(low_incl, 1.0, 0.0).astype(BF16)
    tri_u = jnp.where(cj >= ri, 1.0, 0.0).astype(BF16)

    a_col = -jnp.exp(alog_c_ref[...])

    def exact_right(a, m):
        hi, mid, lo = _split3(a)
        return _dot(hi, m) + _dot(mid, m) + _dot(lo, m)

    def carry(x_chunk, slab_e, slab_w, edge, h_ref):
        x32 = x_chunk[:, :SSM_INNER].astype(F32)
        outs = []
        for g in range(SSM_GROUPS):
            lo_, hi_ = g * GROUP_INNER, (g + 1) * GROUP_INNER
            bm = x_chunk[:, SSM_INNER + g * SSM_STATE:SSM_INNER + (g + 1) * SSM_STATE]
            cm = x_chunk[:, SSM_INNER + SSM_GROUPS * SSM_STATE + g * SSM_STATE:SSM_INNER + SSM_GROUPS * SSM_STATE + (g + 1) * SSM_STATE]
            hprev = h_ref[g]
            outs.append(_dot(cm, hprev.astype(BF16)) * slab_e[:, lo_:hi_])
            xw = (x32[:, lo_:hi_] * slab_w[:, lo_:hi_]).astype(BF16)
            h_ref[g] = hprev * slab_e[edge:edge + 1, lo_:hi_] + _dot_tn(bm, xw)
        return jnp.concatenate(outs, axis=1)

    lt = cj < ri
    gt = cj > ri
    row16 = lax.broadcasted_iota(jnp.int32, (DT_WIDTH, ln), 0)
    trow = lax.broadcasted_iota(jnp.int32, (DT_WIDTH, 2 * ln), 0)
    lane = lax.broadcasted_iota(jnp.int32, (ln, LANES), 1)

    n_sub = xf_ref.shape[1] // ln

    def b_m(x_chunk, g):
        return x_chunk[:, SSM_INNER + g * SSM_STATE:SSM_INNER + (g + 1) * SSM_STATE]

    def c_m(x_chunk, g):
        o = SSM_INNER + SSM_GROUPS * SSM_STATE
        return x_chunk[:, o + g * SSM_STATE:o + (g + 1) * SSM_STATE]

    def prepare(k):
        kb = n_sub - 1 - k
        xc = xf_ref[0, k * ln:(k + 1) * ln, :]
        dt_r = _softplus(dttf_ref[:, k * ln:(k + 1) * ln] + bias_c_ref[...])
        a_r = dt_r * a_col
        cs_r = exact_right(a_r, tri_u)
        sf_r = exact_right(a_r, tri_l)
        dt_b = _softplus(dttb_ref[:, kb * ln:(kb + 1) * ln] + bias_c_ref[...])
        sf_b = exact_right(dt_b * a_col, tri_l)
        last = cs_r[:, ln - 1:ln]
        head = sf_b[:, 0:1]

        x_hi, x_mid, x_lo = _split3(jnp.where(row16 < SSM_HEADS, cs_r, sf_r))
        p_mat = jnp.concatenate([x_hi, x_mid, x_lo, jnp.ones((DT_WIDTH, ln), BF16)], axis=0)
        x_terms = [t.astype(F32) for t in (x_hi, x_mid, x_lo)]
        dds = []
        for hd in range(SSM_HEADS):
            t = jnp.zeros((DT_WIDTH, 2 * ln), F32)
            for term, x in enumerate(x_terms):
                piece = jnp.concatenate([x[hd:hd + 1, :], x[SSM_HEADS + hd:SSM_HEADS + hd + 1, :]], axis=1)
                t = jnp.where(trow == term, -piece, t)
            q_mat = jnp.concatenate([onehot_ref[hd], t.astype(BF16)], axis=0)
            dds.append(_dot_tn(p_mat, q_mat))
        return dict(
            xc=xc, dt=dt_r, dds=dds,
            cb=[_dot_nt(c_m(xc, g), b_m(xc, g)) for g in range(SSM_GROUPS)],
            f_e=_slab(jnp.exp(cs_r), self_ref[...]), f_w=_slab(dt_r * jnp.exp(last - cs_r), self_ref[...]),
            b_e=_slab(jnp.exp(sf_b), selb_ref[...]), b_w=_slab(dt_b * jnp.exp(head - sf_b), selb_ref[...]))

    def finish(k, p):
        kb = n_sub - 1 - k
        xc, dt_r = p["xc"], p["dt"]
        xs = xc[:, :SSM_INNER]
        y_parts = []
        for g in range(SSM_GROUPS):
            for pair in range(HEADS_PER_STATE_GROUP // 2):
                ms = []
                for e in range(2):
                    hd = g * HEADS_PER_STATE_GROUP + pair * 2 + e
                    hb_i = SSM_HEADS + hd
                    dd = p["dds"][hd]
                    ex = jnp.exp(jnp.where(low_incl, dd[:, :ln], dd[:, ln:]))
                    dtf_j = dt_r[hd:hd + 1, :]
                    dtb_j = dt_r[hb_i:hb_i + 1, :]
                    dts = jnp.where(lt, dtf_j, jnp.where(gt, dtb_j, dtf_j + dtb_j))
                    ms.append((p["cb"][g] * ex * dts).astype(BF16))
                col0 = (g * HEADS_PER_STATE_GROUP + pair * 2) * SSM_HEAD_DIM
                xp = xs[:, col0:col0 + LANES]
                zero = jnp.zeros_like(xp)
                xbd = jnp.concatenate([jnp.where(lane < SSM_HEAD_DIM, xp, zero),
                                       jnp.where(lane >= SSM_HEAD_DIM, xp, zero)], axis=0)
                y_parts.append(_dot(jnp.concatenate(ms, axis=1), xbd))
        y = jnp.concatenate(y_parts, axis=1) + xs.astype(F32) * dskip_ref[...]
        ya_ref[0, k * ln:(k + 1) * ln, :] = (y + carry(xc, p["f_e"], p["f_w"], ln - 1, hf_ref)).astype(BF16)
        yb_ref[0, kb * ln:(kb + 1) * ln, :] = carry(xb_ref[0, kb * ln:(kb + 1) * ln, :], p["b_e"], p["b_w"], 0,
                                                    hb_ref).astype(BF16)

    ahead = min(SCAN_AHEAD, n_sub)
    ready = [prepare(k) for k in range(ahead)]
    for k in range(n_sub):
        if k + ahead < n_sub:
            ready.append(prepare(k + ahead))
        finish(k, ready.pop(0))


def _ssd_scan(xc, dtt, w):
    b, s, _ = xc.shape
    ts = TS_SCAN
    nc = s // ts
    fwd = lambda bi, c: (bi, c, 0)
    bwd = lambda bi, c: (bi, nc - 1 - c, 0)
    return pl.pallas_call(
        _ssd_scan_kernel, grid=(b, nc),
        in_specs=[pl.BlockSpec((1, ts, XBC_WIDTH), fwd),
                  pl.BlockSpec((1, ts, XBC_WIDTH), bwd),
                  pl.BlockSpec((DT_WIDTH, ts), lambda bi, c: (0, bi * nc + c)),
                  pl.BlockSpec((DT_WIDTH, ts), lambda bi, c: (0, bi * nc + nc - 1 - c)),
                  _const_spec((DT_WIDTH, 1)), _const_spec((DT_WIDTH, 1)),
                  _const_spec((1, SSM_INNER)),
                  _const_spec((SSM_HEADS, 3 * DT_WIDTH, 2 * SSM_CHUNK)),
                  _const_spec((2 * DT_WIDTH, SSM_INNER)), _const_spec((2 * DT_WIDTH, SSM_INNER))],
        out_specs=[pl.BlockSpec((1, ts, SSM_INNER), fwd), pl.BlockSpec((1, ts, SSM_INNER), bwd)],
        out_shape=[jax.ShapeDtypeStruct((b, s, SSM_INNER), BF16)] * 2,
        scratch_shapes=[pltpu.VMEM((SSM_GROUPS, SSM_STATE, GROUP_INNER), F32)] * 2,
        compiler_params=_params(("parallel", "arbitrary")), name="ssd_scan",
    )(xc, xc, dtt, dtt, w["dtb_c"], w["alog_c"], w["dskip"], w["onehot"], w["sel_f"], w["sel_b"])


def _out_proj_kernel(x_ref, attn_ref, ya_ref, yb_ref, z_ref, mem_ref, sn_ref, wo_ref, o_ref):
    tm = x_ref.shape[0]
    rows_per = tm // OUT_SUBTILES
    for r in range(OUT_SUBTILES):
        rs = slice(r * rows_per, (r + 1) * rows_per)
        y = (ya_ref[rs, :].astype(F32) + yb_ref[rs, :].astype(F32)) * _silu(z_ref[rs, :].astype(F32))
        ms = jnp.mean(y * y, axis=-1, keepdims=True)
        ssm = (y * lax.rsqrt(ms + EPS) * sn_ref[...]).astype(BF16)
        mix = jnp.concatenate([attn_ref[rs, :], mem_ref[rs, :], ssm], axis=1)
        o_ref[rs, :] = x_ref[rs, :] + _dot(mix, wo_ref[...])


def _out_proj(x2d, attn, ya, yb, z, memo, w):
    t = x2d.shape[0]
    tm = TM_OUT
    row = lambda i: (i, 0)
    return pl.pallas_call(
        _out_proj_kernel, grid=(t // tm,),
        in_specs=[pl.BlockSpec((tm, D_MODEL), row), pl.BlockSpec((tm, ATTN_WIDTH), row),
                  pl.BlockSpec((tm, SSM_INNER), row), pl.BlockSpec((tm, SSM_INNER), row),
                  pl.BlockSpec((tm, SSM_INNER), row), pl.BlockSpec((tm, MEM_WIDTH), row),
                  _const_spec((1, SSM_INNER)), _resident_spec((MIX_WIDTH, D_MODEL))],
        out_specs=pl.BlockSpec((tm, D_MODEL), row),
        out_shape=jax.ShapeDtypeStruct((t, D_MODEL), F32),
        compiler_params=_params(("parallel",)), name="out_proj",
    )(x2d, attn, ya, yb, z, memo, w["ssmn"], w["wo"])


def _ffn_kernel(x_ref, prev_ref, next_ref, n2_ref, wg_ref, wu_ref, wd_ref, cw_ref, cb_ref, o_ref,
                g_ref, u_ref, act_ref, acc_ref):
    i = pl.program_id(1)
    ni = pl.num_programs(1)
    tm = x_ref.shape[1]
    n2 = n2_ref[...]

    def norm(v):
        ms = jnp.mean(v * v, axis=-1, keepdims=True)
        return (v * lax.rsqrt(ms + EPS) * n2).astype(BF16)

    hl = HALO_BF16
    pad = FFN_CONV // 2
    ts = FFN_SUB
    n_sub = tm // ts
    n_chunks = D_FF // FF_CHUNK

    hs, keeps = [], []
    for sub in range(n_sub):
        lo = norm(prev_ref[0]) if sub == 0 else norm(x_ref[0, sub * ts - hl:sub * ts, :])
        hi = norm(next_ref[0]) if sub == n_sub - 1 else norm(x_ref[0, (sub + 1) * ts:(sub + 1) * ts + hl, :])
        hs.append(jnp.concatenate([lo, norm(x_ref[0, sub * ts:(sub + 1) * ts, :]), hi], axis=0))
        keeps.append(((i > 0).astype(F32) if sub == 0 else None,
                      (i < ni - 1).astype(F32) if sub == n_sub - 1 else None))

    items = [(c, sub) for c in range(n_chunks) for sub in range(n_sub)]

    def up(it):
        c, sub = items[it]
        c0 = c * FF_CHUNK
        slot = it % 2
        keep_prev, keep_next = keeps[sub]
        for dst, wref in ((g_ref, wg_ref), (u_ref, wu_ref)):
            r = _dot(hs[sub], wref[:, c0:c0 + FF_CHUNK])
            dst[slot, 0:hl, :] = r[0:hl, :] if keep_prev is None else r[0:hl, :] * keep_prev
            dst[slot, hl:hl + ts, :] = r[hl:hl + ts, :]
            dst[slot, hl + ts:, :] = r[hl + ts:, :] if keep_next is None else r[hl + ts:, :] * keep_next

    def conv_act(it):
        c, _ = items[it]
        c0 = c * FF_CHUNK
        slot = it % 2
        outs = []
        for src, off in ((g_ref, 0), (u_ref, D_FF)):
            cw = cw_ref[:, off + c0:off + c0 + FF_CHUNK]
            cb = cb_ref[:, off + c0:off + c0 + FF_CHUNK]
            y = cb + src[slot, pl.ds(hl - pad, ts), :] * cw[0:1, :]
            for t in range(1, FFN_CONV):
                y = y + src[slot, pl.ds(hl - pad + t, ts), :] * cw[t:t + 1, :]
            outs.append(y)
        act_ref[it % ACT_SLOTS] = (_silu(outs[0]) * outs[1]).astype(BF16)

    def down(it):
        c, sub = items[it]
        c0 = c * FF_CHUNK
        part = _dot(act_ref[it % ACT_SLOTS], wd_ref[c0:c0 + FF_CHUNK, :])
        if c == 0:
            acc_ref[sub] = part
        else:
            acc_ref[sub] += part

    lag = ACT_SLOTS - 1
    n_items = len(items)
    done = 0
    up(0)
    for it in range(n_items):
        if it + 1 < n_items:
            up(it + 1)
        target = it - lag + 1 if it + 2 < n_items else it
        while done < target:
            down(done)
            done += 1
        conv_act(it)
    while done < n_items:
        down(done)
        done += 1
    for sub in range(n_sub):
        o_ref[0, sub * ts:(sub + 1) * ts, :] = x_ref[0, sub * ts:(sub + 1) * ts, :] + acc_ref[sub]


def _ffn(x1, w):
    b, s, _ = x1.shape
    tm = TM_FFN
    hl = HALO_BF16
    r = tm // hl
    nh = s // hl
    return pl.pallas_call(
        _ffn_kernel, grid=(b, s // tm),
        in_specs=[pl.BlockSpec((1, tm, D_MODEL), lambda bi, i: (bi, i, 0)),
                  pl.BlockSpec((1, hl, D_MODEL), lambda bi, i: (bi, jnp.maximum(i * r - 1, 0), 0)),
                  pl.BlockSpec((1, hl, D_MODEL), lambda bi, i: (bi, jnp.minimum((i + 1) * r, nh - 1), 0)),
                  _const_spec((1, D_MODEL)),
                  _resident_spec((D_MODEL, D_FF)), _resident_spec((D_MODEL, D_FF)), _resident_spec((D_FF, D_MODEL)),
                  _const_spec((FFN_CONV, 2 * D_FF)), _const_spec((1, 2 * D_FF))],
        out_specs=pl.BlockSpec((1, tm, D_MODEL), lambda bi, i: (bi, i, 0)),
        out_shape=jax.ShapeDtypeStruct((b, s, D_MODEL), F32),
        scratch_shapes=[pltpu.VMEM((2, FFN_SUB + 2 * hl, FF_CHUNK), F32),
                        pltpu.VMEM((2, FFN_SUB + 2 * hl, FF_CHUNK), F32),
                        pltpu.VMEM((ACT_SLOTS, FFN_SUB, FF_CHUNK), BF16),
                        pltpu.VMEM((tm // FFN_SUB, FFN_SUB, D_MODEL), F32)],
        compiler_params=_params(("parallel", "parallel")), name="ffn",
    )(x1, x1, x1, w["n2"], w["wg"], w["wu"], w["wd"], w["fcw"], w["fcb"])


def _prep_weights(max_seq, norm1_w, w_in, q_norm_w, k_norm_w, attn_sink, ssm_conv_w, ssm_conv_b, ssm_dt_bias,
                  ssm_A_log, ssm_D, ssm_norm_w, mem_norm_w, w_mem_kv, mq_norm_w, mk_norm_w, w_out, norm2_w,
                  w_ffn_up, ffn_conv_w, ffn_conv_b, w_ffn_down):
    o = 0
    wq = w_in[:, o:o + ATTN_WIDTH]; o += ATTN_WIDTH
    wk = w_in[:, o:o + KV_WIDTH]; o += KV_WIDTH
    wv = w_in[:, o:o + KV_WIDTH]; o += KV_WIDTH
    wz = w_in[:, o:o + SSM_INNER]; o += SSM_INNER
    wx = w_in[:, o:o + XBC_WIDTH]; o += XBC_WIDTH
    wdt = w_in[:, o:o + DT_WIDTH]; o += DT_WIDTH
    wmq = w_in[:, o:o + MEM_WIDTH]

    wq_p = wq.reshape(D_MODEL, N_KV_HEADS, GQA_GROUP, 2, HALF).transpose(0, 2, 3, 1, 4).reshape(D_MODEL, ATTN_WIDTH)
    wk_p = wk.reshape(D_MODEL, N_KV_HEADS, 2, HALF).transpose(0, 2, 1, 3).reshape(D_MODEL, KV_WIDTH)
    qg = jnp.broadcast_to(q_norm_w.reshape(1, 2, 1, HALF), (GQA_GROUP, 2, N_KV_HEADS, HALF)).reshape(ATTN_WIDTH)
    kg = jnp.broadcast_to(k_norm_w.reshape(2, 1, HALF), (2, N_KV_HEADS, HALF)).reshape(KV_WIDTH)
    qkg = jnp.concatenate([qg * (HEAD_DIM ** -0.5 * LOG2E), kg]).reshape(1, QK_WIDTH)

    cq = np.arange(ATTN_WIDTH)
    eq = (cq // 256) * N_KV_HEADS + (cq % LANES) // HALF
    ck = np.arange(KV_WIDTH)
    ek = N_Q_HEADS + (ck % LANES) // HALF
    e_all = np.concatenate([eq, ek])
    hsum = np.zeros((QK_WIDTH, LANES), np.float32)
    hsum[np.arange(QK_WIDTH), e_all] = 1.0
    hexp = np.concatenate([hsum.T, hsum.T], axis=0)

    inv = ROPE_THETA ** (-jnp.arange(0, HEAD_DIM, 2, dtype=F32) / HEAD_DIM)
    ang = jnp.arange(max_seq, dtype=F32)[:, None] * inv[None, :]
    cos = jnp.tile(jnp.cos(ang), (1, LANES // HALF))
    sin = jnp.tile(jnp.sin(ang), (1, LANES // HALF))

    wo_a = w_out[:ATTN_WIDTH].reshape(N_KV_HEADS, GQA_GROUP, HEAD_DIM, D_MODEL).transpose(1, 0, 2, 3)
    wo_a = wo_a.reshape(ATTN_WIDTH, D_MODEL)

    r48 = np.arange(3 * DT_WIDTH) % DT_WIDTH
    onehot = np.zeros((SSM_HEADS, 3 * DT_WIDTH, 2 * SSM_CHUNK), np.float32)
    for hd in range(SSM_HEADS):
        onehot[hd, r48 == hd, :SSM_CHUNK] = 1.0
        onehot[hd, r48 == SSM_HEADS + hd, SSM_CHUNK:] = 1.0
    r32 = np.arange(2 * DT_WIDTH) % DT_WIDTH
    col_head = np.arange(SSM_INNER) // SSM_HEAD_DIM
    sel_f = (r32[:, None] == col_head[None, :]).astype(np.float32)
    sel_b = (r32[:, None] == SSM_HEADS + col_head[None, :]).astype(np.float32)
    return {
        "onehot": jnp.asarray(onehot, BF16), "sel_f": jnp.asarray(sel_f, BF16), "sel_b": jnp.asarray(sel_b, BF16),
        "n1": norm1_w.reshape(1, D_MODEL),
        "wqk": jnp.concatenate([wq_p, wk_p], axis=1).astype(BF16),
        "wv": wv.astype(BF16), "wz": wz.astype(BF16), "wx": wx.astype(BF16),
        "wdtt": wdt.T.astype(BF16), "wmq": wmq.astype(BF16),
        "hsum": jnp.asarray(hsum, BF16), "hexp": jnp.asarray(hexp, BF16),
        "qkg": qkg, "mqg": mq_norm_w.reshape(1, LANES), "cos": cos, "sin": sin,
        "sink": attn_sink,
        "memn": mem_norm_w.reshape(1, D_MODEL), "wmemkv": w_mem_kv.astype(BF16), "mkg": mk_norm_w.reshape(1, LANES),
        "convw": ssm_conv_w, "convb": ssm_conv_b.reshape(1, XBC_WIDTH),
        "dtb_c": ssm_dt_bias.reshape(DT_WIDTH, 1), "alog_c": ssm_A_log.reshape(DT_WIDTH, 1),
        "dskip": jnp.repeat(ssm_D, SSM_HEAD_DIM).reshape(1, SSM_INNER),
        "ssmn": ssm_norm_w.reshape(1, SSM_INNER),
        "wo": jnp.concatenate([wo_a, w_out[ATTN_WIDTH + SSM_INNER:],
                               w_out[ATTN_WIDTH:ATTN_WIDTH + SSM_INNER]], axis=0).astype(BF16),
        "n2": norm2_w.reshape(1, D_MODEL),
        "wg": w_ffn_up[:, :D_FF].astype(BF16), "wu": w_ffn_up[:, D_FF:].astype(BF16),
        "wd": w_ffn_down.astype(BF16),
        "fcw": ffn_conv_w, "fcb": ffn_conv_b.reshape(1, 2 * D_FF),
    }


def _encoder_layer(x, mem, w):
    b, s, _ = x.shape
    t = b * s
    x2d = x.reshape(t, D_MODEL)
    q, k, v, z, xbc, dtt, mq = _in_proj(x2d, s, w)
    mk, mv = _mem_kv(mem, w)
    attn, memo = _attention(q.reshape(b, s, -1), k.reshape(b, s, -1), v.reshape(b, s, -1),
                            mq.reshape(b, s, -1), mk, mv, w["sink"])
    ya, yb = _ssd_scan(xbc.reshape(b, s, -1), dtt, w)
    x1 = _out_proj(x2d, attn.reshape(t, -1), ya.reshape(t, -1), yb.reshape(t, -1), z, memo.reshape(t, -1), w)
    return _ffn(x1.reshape(b, s, D_MODEL), w)


def kernel(x_prompt, x_sample, mem_prompt, mem_sample, norm1_w, w_in, q_norm_w, k_norm_w, attn_sink, ssm_conv_w, ssm_conv_b, ssm_dt_bias, ssm_A_log, ssm_D, ssm_norm_w, mem_norm_w, w_mem_kv, mq_norm_w, mk_norm_w, w_out, norm2_w, w_ffn_up, ffn_conv_w, ffn_conv_b, w_ffn_down):
    weights = (norm1_w, w_in, q_norm_w, k_norm_w, attn_sink, ssm_conv_w, ssm_conv_b, ssm_dt_bias,
               ssm_A_log, ssm_D, ssm_norm_w, mem_norm_w, w_mem_kv, mq_norm_w, mk_norm_w, w_out,
               norm2_w, w_ffn_up, ffn_conv_w, ffn_conv_b, w_ffn_down)
    depth = norm1_w.shape[0]
    max_seq = max(x_prompt.shape[1], x_sample.shape[1])
    y_prompt, y_sample = x_prompt, x_sample
    for layer in range(depth):
        w = _prep_weights(max_seq, *[p[layer] for p in weights])
        y_prompt = _encoder_layer(y_prompt, mem_prompt, w)
        y_sample = _encoder_layer(y_sample, mem_sample, w)
    return (y_prompt, y_sample)
```

```python
import functools
import math

import numpy as np
import jax
import jax.numpy as jnp
from jax import lax
from jax.experimental import pallas as pl
from jax.experimental.pallas import tpu as pltpu

F32 = jnp.float32
BF16 = jnp.bfloat16

D_MODEL = 1024
HEAD_DIM = 64
HALF = HEAD_DIM // 2
N_Q_HEADS = 16
N_KV_HEADS = 4
GQA_GROUP = N_Q_HEADS // N_KV_HEADS
ATTN_WIDTH = N_Q_HEADS * HEAD_DIM
KV_WIDTH = N_KV_HEADS * HEAD_DIM
WINDOW = 128
BLOCK = 128
ROPE_THETA = 10000.0
SSM_HEADS = 8
SSM_HEAD_DIM = 64
SSM_INNER = SSM_HEADS * SSM_HEAD_DIM
SSM_GROUPS = 2
SSM_STATE = 128
SSM_CONV = 5
SSM_CHUNK = 128
XBC_WIDTH = SSM_INNER + 2 * SSM_GROUPS * SSM_STATE
DT_WIDTH = 2 * SSM_HEADS
N_MEM = 256
MEM_HEADS = 4
MEM_HEAD_DIM = 128
MEM_WIDTH = MEM_HEADS * MEM_HEAD_DIM
MIX_WIDTH = ATTN_WIDTH + SSM_INNER + MEM_WIDTH
D_FF = 2816
FFN_CONV = 3
EPS = 1e-6

LANES = 128
QK_WIDTH = ATTN_WIDTH + KV_WIDTH
HEADS_PER_STATE_GROUP = SSM_HEADS // SSM_GROUPS
GROUP_INNER = HEADS_PER_STATE_GROUP * SSM_HEAD_DIM
NEG_BIG = -1e30
LOG2E = math.log2(math.e)
VMEM_LIMIT = 56 * 1024 * 1024

TM_PROJ = 1024
XBC_PIECES = 4
TQ_ATTN = 1024
TC_CONV = 512
SCAN_AHEAD = 1
TS_SCAN = 1024
TM_OUT = 1024
OUT_SUBTILES = 2
TM_FFN = 512
FFN_SUB = 512
FF_CHUNK = 256
ACT_SLOTS = 3
HALO = 8
HALO_BF16 = 16


def _dot(a, b):
    return jnp.dot(a, b, preferred_element_type=F32)


def _dot_nt(a, b):
    return lax.dot_general(a, b, (((1,), (1,)), ((), ())), preferred_element_type=F32)


def _dot_tn(a, b):
    return lax.dot_general(a, b, (((0,), (0,)), ((), ())), preferred_element_type=F32)


def _split3(a):
    hi = a.astype(BF16)
    r = a - hi.astype(F32)
    mid = r.astype(BF16)
    lo = (r - mid.astype(F32)).astype(BF16)
    return hi, mid, lo


def _silu(x):
    return x / (1.0 + jnp.exp(-x))


def _softplus(x):
    return jnp.maximum(x, 0.0) + jnp.log1p(jnp.exp(-jnp.abs(x)))


def _shift_rows(xe, off, n, delta):
    cur = xe[off:off + n]
    if delta == 0:
        return cur
    r = lax.broadcasted_iota(jnp.int32, cur.shape, 0) % HALO
    if delta < 0:
        merged = jnp.where(r < HALO + delta, cur, xe[off - HALO:off + n - HALO])
        rot = -delta
    else:
        merged = jnp.where(r >= delta, cur, xe[off + HALO:off + n + HALO])
        rot = HALO - delta
    width = cur.shape[1]
    return pltpu.roll(merged.reshape(n // HALO, HALO, width), rot, axis=1).reshape(n, width)


def _params(sem):
    return pltpu.CompilerParams(dimension_semantics=sem, vmem_limit_bytes=VMEM_LIMIT)


def _const_spec(shape):
    nd = len(shape)
    return pl.BlockSpec(shape, lambda *_: (0,) * nd)


def _resident_spec(shape):
    nd = len(shape)
    return pl.BlockSpec(shape, lambda *_: (0,) * nd, pipeline_mode=pl.Buffered(1))


def _in_proj_kernel(x_ref, xprev_ref, xnext_ref, n1_ref, wqk_ref, wv_ref, wz_ref, wx_ref, wdtt_ref, wmq_ref,
                    hsum_ref, hexp_ref, qkg_ref, mqg_ref, cos_ref, sin_ref, convw_ref, convb_ref,
                    q_ref, k_ref, v_ref, z_ref, xbc_ref, dtt_ref, mq_ref, xe_ref, *, blocks_per_seq):
    tm = x_ref.shape[0]
    n1 = n1_ref[...]

    def norm(x):
        ms = jnp.mean(x * x, axis=-1, keepdims=True)
        return (x * lax.rsqrt(ms + EPS) * n1).astype(BF16)

    h = norm(x_ref[...])

    pos = pl.program_id(0) % blocks_per_seq
    hb = HALO_BF16
    pad = SSM_CONV // 2
    hext = jnp.concatenate([norm(xprev_ref[...]), h, norm(xnext_ref[...])], axis=0)
    keep_prev = (pos > 0).astype(F32)
    keep_next = (pos < blocks_per_seq - 1).astype(F32)

    piece = XBC_WIDTH // XBC_PIECES

    def project(p):
        r = _dot(hext, wx_ref[:, p * piece:(p + 1) * piece])
        slot = p % 2
        xe_ref[slot, :hb, :] = r[:hb] * keep_prev
        xe_ref[slot, hb:hb + tm, :] = r[hb:hb + tm]
        xe_ref[slot, hb + tm:, :] = r[hb + tm:] * keep_next

    def conv(p):
        c0, c1, slot = p * piece, (p + 1) * piece, p % 2
        y = convb_ref[:, c0:c1]
        for t in range(SSM_CONV):
            y = y + xe_ref[slot, pl.ds(hb - pad + t, tm), :] * convw_ref[t:t + 1, c0:c1]
        xbc_ref[:, c0:c1] = _silu(y).astype(BF16)

    qk = _dot(h, wqk_ref[...])
    project(0)
    project(1)
    v_ref[...] = _dot(h, wv_ref[...]).astype(BF16)
    ssq = _dot((qk * qk).astype(BF16), hsum_ref[...])
    conv(0)
    project(2)
    z_ref[...] = _dot(h, wz_ref[...]).astype(BF16)
    inv = lax.rsqrt(ssq * (1.0 / HEAD_DIM) + EPS)
    inv_hi = inv.astype(BF16)
    inv_lo = (inv - inv_hi.astype(F32)).astype(BF16)
    scale = _dot(jnp.concatenate([inv_hi, inv_lo], axis=1), hexp_ref[...])
    conv(1)
    project(3)
    mq = _dot(h, wmq_ref[...])
    dtt_ref[...] = _dot_nt(wdtt_ref[...], h)
    conv(2)
    conv(3)
    mqg = mqg_ref[...] * (MEM_HEAD_DIM ** -0.5 * LOG2E)
    for hh in range(MEM_HEADS):
        m = mq[:, hh * LANES:(hh + 1) * LANES]
        r = lax.rsqrt(jnp.mean(m * m, axis=-1, keepdims=True) + EPS)
        mq_ref[:, hh * LANES:(hh + 1) * LANES] = (m * r * mqg).astype(BF16)

    qkn = qk * scale * qkg_ref[...]

    cos = cos_ref[...]
    sin = sin_ref[...]
    for p in range(GQA_GROUP):
        u = qkn[:, p * 256:p * 256 + LANES]
        w = qkn[:, p * 256 + LANES:(p + 1) * 256]
        q_ref[:, p * 256:p * 256 + LANES] = (u * cos - w * sin).astype(BF16)
        q_ref[:, p * 256 + LANES:(p + 1) * 256] = (w * cos + u * sin).astype(BF16)
    u = qkn[:, ATTN_WIDTH:ATTN_WIDTH + LANES]
    w = qkn[:, ATTN_WIDTH + LANES:QK_WIDTH]
    k_ref[:, :LANES] = (u * cos - w * sin).astype(BF16)
    k_ref[:, LANES:] = (w * cos + u * sin).astype(BF16)


def _in_proj(x2d, s, w):
    t = x2d.shape[0]
    tm = TM_PROJ
    nblk_seq = s // tm
    row = lambda i: (i, 0)
    tab = lambda i: (i % nblk_seq, 0)
    r = tm // HALO_BF16
    nh = t // HALO_BF16
    in_specs = [
        pl.BlockSpec((tm, D_MODEL), row),
        pl.BlockSpec((HALO_BF16, D_MODEL), lambda i: (jnp.maximum(i * r - 1, 0), 0)),
        pl.BlockSpec((HALO_BF16, D_MODEL), lambda i: (jnp.minimum((i + 1) * r, nh - 1), 0)),
        _const_spec((1, D_MODEL)),
        _resident_spec((D_MODEL, QK_WIDTH)),
        _resident_spec((D_MODEL, KV_WIDTH)),
        _resident_spec((D_MODEL, SSM_INNER)),
        _resident_spec((D_MODEL, XBC_WIDTH)),
        _const_spec((DT_WIDTH, D_MODEL)),
        _resident_spec((D_MODEL, MEM_WIDTH)),
        _const_spec((QK_WIDTH, LANES)),
        _const_spec((2 * LANES, QK_WIDTH)),
        _const_spec((1, QK_WIDTH)),
        _const_spec((1, LANES)),
        pl.BlockSpec((tm, LANES), tab),
        pl.BlockSpec((tm, LANES), tab),
        _const_spec((SSM_CONV, XBC_WIDTH)),
        _const_spec((1, XBC_WIDTH)),
    ]
    out_shape = [
        jax.ShapeDtypeStruct((t, ATTN_WIDTH), BF16),
        jax.ShapeDtypeStruct((t, KV_WIDTH), BF16),
        jax.ShapeDtypeStruct((t, KV_WIDTH), BF16),
        jax.ShapeDtypeStruct((t, SSM_INNER), BF16),
        jax.ShapeDtypeStruct((t, XBC_WIDTH), BF16),
        jax.ShapeDtypeStruct((DT_WIDTH, t), F32),
        jax.ShapeDtypeStruct((t, MEM_WIDTH), BF16),
    ]
    out_specs = [
        pl.BlockSpec((tm, ATTN_WIDTH), row),
        pl.BlockSpec((tm, KV_WIDTH), row),
        pl.BlockSpec((tm, KV_WIDTH), row),
        pl.BlockSpec((tm, SSM_INNER), row),
        pl.BlockSpec((tm, XBC_WIDTH), row),
        pl.BlockSpec((DT_WIDTH, tm), lambda i: (0, i)),
        pl.BlockSpec((tm, MEM_WIDTH), row),
    ]
    return pl.pallas_call(
        functools.partial(_in_proj_kernel, blocks_per_seq=nblk_seq),
        grid=(t // tm,), in_specs=in_specs, out_specs=out_specs, out_shape=out_shape,
        scratch_shapes=[pltpu.VMEM((2, tm + 2 * HALO_BF16, XBC_WIDTH // XBC_PIECES), F32)],
        compiler_params=_params(("parallel",)), name="in_proj",
    )(x2d, x2d, x2d, w["n1"], w["wqk"], w["wv"], w["wz"], w["wx"], w["wdtt"], w["wmq"],
      w["hsum"], w["hexp"], w["qkg"], w["mqg"], w["cos"][:s], w["sin"][:s], w["convw"], w["convb"])


def _mem_kv_kernel(mem_ref, nw_ref, w_ref, mkg_ref, mk_ref, mv_ref):
    x = mem_ref[0]
    ms = jnp.mean(x * x, axis=-1, keepdims=True)
    h = (x * lax.rsqrt(ms + EPS) * nw_ref[...]).astype(BF16)
    kv = _dot(h, w_ref[...])
    g = mkg_ref[...]
    for hh in range(MEM_HEADS):
        m = kv[:, hh * LANES:(hh + 1) * LANES]
        r = lax.rsqrt(jnp.mean(m * m, axis=-1, keepdims=True) + EPS)
        mk_ref[0, :, hh * LANES:(hh + 1) * LANES] = (m * r * g).astype(BF16)
    mv_ref[0] = kv[:, MEM_WIDTH:].astype(BF16)


def _mem_kv(mem, w):
    b = mem.shape[0]
    blk = lambda i: (i, 0, 0)
    return pl.pallas_call(
        _mem_kv_kernel, grid=(b,),
        in_specs=[pl.BlockSpec((1, N_MEM, D_MODEL), blk), _const_spec((1, D_MODEL)),
                  _const_spec((D_MODEL, 2 * MEM_WIDTH)), _const_spec((1, LANES))],
        out_specs=[pl.BlockSpec((1, N_MEM, MEM_WIDTH), blk), pl.BlockSpec((1, N_MEM, MEM_WIDTH), blk)],
        out_shape=[jax.ShapeDtypeStruct((b, N_MEM, MEM_WIDTH), BF16)] * 2,
        compiler_params=_params(("parallel",)), name="mem_kv",
    )(mem, w["memn"], w["wmemkv"], w["mkg"])


def _attention_kernel(sink_ref, q_ref, k_ref, v_ref, mq_ref, mk_ref, mv_ref, o_ref, mo_ref, *, seq):
    qi = pl.program_id(1)
    tq = q_ref.shape[1]
    kwin = 3 * BLOCK

    lane_k = lax.broadcasted_iota(jnp.int32, (kwin, 2 * LANES), 1)
    lane_o = lax.broadcasted_iota(jnp.int32, (BLOCK, 2 * LANES), 1)
    row_i = lax.broadcasted_iota(jnp.int32, (BLOCK, kwin), 0)
    col_i = lax.broadcasted_iota(jnp.int32, (BLOCK, kwin), 1)

    windows = {}

    def window(blk):
        if blk not in windows:
            r0 = qi * tq + blk * BLOCK
            ks = pl.multiple_of(jnp.clip(r0 - BLOCK, 0, seq - kwin), BLOCK)
            kw = k_ref[0, pl.ds(ks, kwin), :]
            vw = v_ref[0, pl.ds(ks, kwin), :]
            delta = (col_i - row_i) + (ks - r0)
            bias = jnp.where(jnp.abs(delta) <= WINDOW, 0.0, NEG_BIG).astype(F32)
            zero = jnp.zeros_like(kw)
            kj = [jnp.where((lane_k % LANES) // HALF == j, kw, zero) for j in range(N_KV_HEADS)]
            kpairs = [jnp.concatenate(kj[2 * i:2 * i + 2], axis=0) for i in range(N_KV_HEADS // 2)]
            v4 = jnp.concatenate([jnp.where(lane_k // HEAD_DIM == j, vw, zero) for j in range(N_KV_HEADS)], axis=0)
            windows[blk] = (bias, kpairs, v4)
        return windows[blk]

    def logits(item):
        if item[0] == "w":
            _, blk, p = item
            _, kpairs, _ = window(blk)
            qp = q_ref[0, blk * BLOCK:(blk + 1) * BLOCK, p * 256:(p + 1) * 256]
            pairs = [_dot_nt(qp, kp) for kp in kpairs]
            return [pr[:, i * kwin:(i + 1) * kwin] for pr in pairs for i in range(2)]
        hh = item[1]
        return _dot_nt(mq_ref[0, :, hh * LANES:(hh + 1) * LANES], mk_ref[0, :, hh * LANES:(hh + 1) * LANES])

    def softmax(item, sc):
        if item[0] == "w":
            _, blk, p = item
            bias, _, _ = window(blk)
            probs, invs = [], []
            for j in range(N_KV_HEADS):
                snk = sink_ref[j * GQA_GROUP + p] * LOG2E
                s = sc[j] + bias
                m = jnp.maximum(jnp.max(s, axis=-1, keepdims=True), snk)
                e = jnp.exp2(s - m)
                den = jnp.sum(e, axis=-1, keepdims=True) + jnp.exp2(snk - m)
                probs.append(e.astype(BF16))
                invs.append(1.0 / den)
            inv = jnp.where(lane_o < HEAD_DIM, invs[0],
                            jnp.where(lane_o < 2 * HEAD_DIM, invs[1],
                                      jnp.where(lane_o < 3 * HEAD_DIM, invs[2], invs[3])))
            return jnp.concatenate(probs, axis=1), inv
        m = jnp.max(sc, axis=-1, keepdims=True)
        e = jnp.exp2(sc - m)
        return e.astype(BF16), 1.0 / jnp.sum(e, axis=-1, keepdims=True)

    def values(item, pr):
        probs, inv = pr
        if item[0] == "w":
            _, blk, p = item
            _, _, v4 = window(blk)
            o_ref[0, blk * BLOCK:(blk + 1) * BLOCK, p * 256:(p + 1) * 256] = (_dot(probs, v4) * inv).astype(BF16)
        else:
            hh = item[1]
            o = _dot(probs, mv_ref[0, :, hh * LANES:(hh + 1) * LANES])
            mo_ref[0, :, hh * LANES:(hh + 1) * LANES] = (o * inv).astype(BF16)

    items = [("w", blk, p) for blk in range(tq // BLOCK) for p in range(GQA_GROUP)]
    items += [("m", hh) for hh in range(MEM_HEADS)]
    n = len(items)
    sc_next = logits(items[0])
    pr_prev = None
    for i in range(n):
        sc_cur = sc_next
        if i + 1 < n:
            sc_next = logits(items[i + 1])
        if pr_prev is not None:
            values(items[i - 1], pr_prev)
        pr_prev = softmax(items[i], sc_cur)
    values(items[n - 1], pr_prev)


def _attention(q, k, v, mq, mk, mv, sink):
    b, s, _ = q.shape
    tq = TQ_ATTN
    qblk = lambda bi, i: (bi, i, 0)
    full = lambda bi, i: (bi, 0, 0)
    return pl.pallas_call(
        functools.partial(_attention_kernel, seq=s), grid=(b, s // tq),
        in_specs=[pl.BlockSpec(memory_space=pltpu.SMEM),
                  pl.BlockSpec((1, tq, ATTN_WIDTH), qblk),
                  pl.BlockSpec((1, s, KV_WIDTH), full),
                  pl.BlockSpec((1, s, KV_WIDTH), full),
                  pl.BlockSpec((1, tq, MEM_WIDTH), qblk),
                  pl.BlockSpec((1, N_MEM, MEM_WIDTH), full),
                  pl.BlockSpec((1, N_MEM, MEM_WIDTH), full)],
        out_specs=[pl.BlockSpec((1, tq, ATTN_WIDTH), qblk), pl.BlockSpec((1, tq, MEM_WIDTH), qblk)],
        out_shape=[jax.ShapeDtypeStruct((b, s, ATTN_WIDTH), BF16), jax.ShapeDtypeStruct((b, s, MEM_WIDTH), BF16)],
        compiler_params=_params(("parallel", "arbitrary")), name="attention",
    )(sink, q, k, v, mq, mk, mv)


def _slab(rows, sel):
    hi = rows.astype(BF16)
    lo = (rows - hi.astype(F32)).astype(BF16)
    return _dot_tn(jnp.concatenate([hi, lo], axis=0), sel)


def _ssd_scan_kernel(xf_ref, xb_ref, dttf_ref, dttb_ref, bias_c_ref, alog_c_ref, dskip_ref,
                     onehot_ref, self_ref, selb_ref, ya_ref, yb_ref, hf_ref, hb_ref):
    c = pl.program_id(1)

    @pl.when(c == 0)
    def _():
        hf_ref[...] = jnp.zeros_like(hf_ref)
        hb_ref[...] = jnp.zeros_like(hb_ref)

    ln = SSM_CHUNK
    ri = lax.broadcasted_iota(jnp.int32, (ln, ln), 0)
    cj = lax.broadcasted_iota(jnp.int32, (ln, ln), 1)
    low_incl = (cj <= ri)
    tri_l = jnp.where(low_incl, 1.0, 0.0).astype(BF16)
    tri_u = jnp.where(cj >= ri, 1.0, 0.0).astype(BF16)

    a_col = -jnp.exp(alog_c_ref[...])

    def exact_right(a, m):
        hi, mid, lo = _split3(a)
        return _dot(hi, m) + _dot(mid, m) + _dot(lo, m)

    def carry(x_chunk, slab_e, slab_w, edge, h_ref):
        x32 = x_chunk[:, :SSM_INNER].astype(F32)
        outs = []
        for g in range(SSM_GROUPS):
            lo_, hi_ = g * GROUP_INNER, (g + 1) * GROUP_INNER
            bm = x_chunk[:, SSM_INNER + g * SSM_STATE:SSM_INNER + (g + 1) * SSM_STATE]
            cm = x_chunk[:, SSM_INNER + SSM_GROUPS * SSM_STATE + g * SSM_STATE:SSM_INNER + SSM_GROUPS * SSM_STATE + (g + 1) * SSM_STATE]
            hprev = h_ref[g]
            outs.append(_dot(cm, hprev.astype(BF16)) * slab_e[:, lo_:hi_])
            xw = (x32[:, lo_:hi_] * slab_w[:, lo_:hi_]).astype(BF16)
            h_ref[g] = hprev * slab_e[edge:edge + 1, lo_:hi_] + _dot_tn(bm, xw)
        return jnp.concatenate(outs, axis=1)

    lt = cj < ri
    gt = cj > ri
    row16 = lax.broadcasted_iota(jnp.int32, (DT_WIDTH, ln), 0)
    trow = lax.broadcasted_iota(jnp.int32, (DT_WIDTH, 2 * ln), 0)
    lane = lax.broadcasted_iota(jnp.int32, (ln, LANES), 1)

    n_sub = xf_ref.shape[1] // ln

    def b_m(x_chunk, g):
        return x_chunk[:, SSM_INNER + g * SSM_STATE:SSM_INNER + (g + 1) * SSM_STATE]

    def c_m(x_chunk, g):
        o = SSM_INNER + SSM_GROUPS * SSM_STATE
        return x_chunk[:, o + g * SSM_STATE:o + (g + 1) * SSM_STATE]

    def prepare(k):
        kb = n_sub - 1 - k
        xc = xf_ref[0, k * ln:(k + 1) * ln, :]
        dt_r = _softplus(dttf_ref[:, k * ln:(k + 1) * ln] + bias_c_ref[...])
        a_r = dt_r * a_col
        cs_r = exact_right(a_r, tri_u)
        sf_r = exact_right(a_r, tri_l)
        dt_b = _softplus(dttb_ref[:, kb * ln:(kb + 1) * ln] + bias_c_ref[...])
        sf_b = exact_right(dt_b * a_col, tri_l)
        last = cs_r[:, ln - 1:ln]
        head = sf_b[:, 0:1]

        x_hi, x_mid, x_lo = _split3(jnp.where(row16 < SSM_HEADS, cs_r, sf_r))
        p_mat = jnp.concatenate([x_hi, x_mid, x_lo, jnp.ones((DT_WIDTH, ln), BF16)], axis=0)
        x_terms = [t.astype(F32) for t in (x_hi, x_mid, x_lo)]
        dds = []
        for hd in range(SSM_HEADS):
            t = jnp.zeros((DT_WIDTH, 2 * ln), F32)
            for term, x in enumerate(x_terms):
                piece = jnp.concatenate([x[hd:hd + 1, :], x[SSM_HEADS + hd:SSM_HEADS + hd + 1, :]], axis=1)
                t = jnp.where(trow == term, -piece, t)
            q_mat = jnp.concatenate([onehot_ref[hd], t.astype(BF16)], axis=0)
            dds.append(_dot_tn(p_mat, q_mat))
        return dict(
            xc=xc, dt=dt_r, dds=dds,
            cb=[_dot_nt(c_m(xc, g), b_m(xc, g)) for g in range(SSM_GROUPS)],
            f_e=_slab(jnp.exp(cs_r), self_ref[...]), f_w=_slab(dt_r * jnp.exp(last - cs_r), self_ref[...]),
            b_e=_slab(jnp.exp(sf_b), selb_ref[...]), b_w=_slab(dt_b * jnp.exp(head - sf_b), selb_ref[...]))

    def finish(k, p):
        kb = n_sub - 1 - k
        xc, dt_r = p["xc"], p["dt"]
        xs = xc[:, :SSM_INNER]
        y_parts = []
        for g in range(SSM_GROUPS):
            for pair in range(HEADS_PER_STATE_GROUP // 2):
                ms = []
                for e in range(2):
                    hd = g * HEADS_PER_STATE_GROUP + pair * 2 + e
                    hb_i = SSM_HEADS + hd
                    dd = p["dds"][hd]
                    ex = jnp.exp(jnp.where(low_incl, dd[:, :ln], dd[:, ln:]))
                    dtf_j = dt_r[hd:hd + 1, :]
                    dtb_j = dt_r[hb_i:hb_i + 1, :]
                    dts = jnp.where(lt, dtf_j, jnp.where(gt, dtb_j, dtf_j + dtb_j))
                    ms.append((p["cb"][g] * ex * dts).astype(BF16))
                col0 = (g * HEADS_PER_STATE_GROUP + pair * 2) * SSM_HEAD_DIM
                xp = xs[:, col0:col0 + LANES]
                zero = jnp.zeros_like(xp)
                xbd = jnp.concatenate([jnp.where(lane < SSM_HEAD_DIM, xp, zero),
                                       jnp.where(lane >= SSM_HEAD_DIM, xp, zero)], axis=0)
                y_parts.append(_dot(jnp.concatenate(ms, axis=1), xbd))
        y = jnp.concatenate(y_parts, axis=1) + xs.astype(F32) * dskip_ref[...]
        ya_ref[0, k * ln:(k + 1) * ln, :] = (y + carry(xc, p["f_e"], p["f_w"], ln - 1, hf_ref)).astype(BF16)
        yb_ref[0, kb * ln:(kb + 1) * ln, :] = carry(xb_ref[0, kb * ln:(kb + 1) * ln, :], p["b_e"], p["b_w"], 0,
                                                    hb_ref).astype(BF16)

    ahead = min(SCAN_AHEAD, n_sub)
    ready = [prepare(k) for k in range(ahead)]
    for k in range(n_sub):
        if k + ahead < n_sub:
            ready.append(prepare(k + ahead))
        finish(k, ready.pop(0))


def _ssd_scan(xc, dtt, w):
    b, s, _ = xc.shape
    ts = TS_SCAN
    nc = s // ts
    fwd = lambda bi, c: (bi, c, 0)
    bwd = lambda bi, c: (bi, nc - 1 - c, 0)
    return pl.pallas_call(
        _ssd_scan_kernel, grid=(b, nc),
        in_specs=[pl.BlockSpec((1, ts, XBC_WIDTH), fwd),
                  pl.BlockSpec((1, ts, XBC_WIDTH), bwd),
                  pl.BlockSpec((DT_WIDTH, ts), lambda bi, c: (0, bi * nc + c)),
                  pl.BlockSpec((DT_WIDTH, ts), lambda bi, c: (0, bi * nc + nc - 1 - c)),
                  _const_spec((DT_WIDTH, 1)), _const_spec((DT_WIDTH, 1)),
                  _const_spec((1, SSM_INNER)),
                  _const_spec((SSM_HEADS, 3 * DT_WIDTH, 2 * SSM_CHUNK)),
                  _const_spec((2 * DT_WIDTH, SSM_INNER)), _const_spec((2 * DT_WIDTH, SSM_INNER))],
        out_specs=[pl.BlockSpec((1, ts, SSM_INNER), fwd), pl.BlockSpec((1, ts, SSM_INNER), bwd)],
        out_shape=[jax.ShapeDtypeStruct((b, s, SSM_INNER), BF16)] * 2,
        scratch_shapes=[pltpu.VMEM((SSM_GROUPS, SSM_STATE, GROUP_INNER), F32)] * 2,
        compiler_params=_params(("parallel", "arbitrary")), name="ssd_scan",
    )(xc, xc, dtt, dtt, w["dtb_c"], w["alog_c"], w["dskip"], w["onehot"], w["sel_f"], w["sel_b"])


def _out_proj_kernel(x_ref, attn_ref, ya_ref, yb_ref, z_ref, mem_ref, sn_ref, wo_ref, o_ref):
    tm = x_ref.shape[0]
    rows_per = tm // OUT_SUBTILES
    for r in range(OUT_SUBTILES):
        rs = slice(r * rows_per, (r + 1) * rows_per)
        y = (ya_ref[rs, :].astype(F32) + yb_ref[rs, :].astype(F32)) * _silu(z_ref[rs, :].astype(F32))
        ms = jnp.mean(y * y, axis=-1, keepdims=True)
        ssm = (y * lax.rsqrt(ms + EPS) * sn_ref[...]).astype(BF16)
        mix = jnp.concatenate([attn_ref[rs, :], mem_ref[rs, :], ssm], axis=1)
        o_ref[rs, :] = x_ref[rs, :] + _dot(mix, wo_ref[...])


def _out_proj(x2d, attn, ya, yb, z, memo, w):
    t = x2d.shape[0]
    tm = TM_OUT
    row = lambda i: (i, 0)
    return pl.pallas_call(
        _out_proj_kernel, grid=(t // tm,),
        in_specs=[pl.BlockSpec((tm, D_MODEL), row), pl.BlockSpec((tm, ATTN_WIDTH), row),
                  pl.BlockSpec((tm, SSM_INNER), row), pl.BlockSpec((tm, SSM_INNER), row),
                  pl.BlockSpec((tm, SSM_INNER), row), pl.BlockSpec((tm, MEM_WIDTH), row),
                  _const_spec((1, SSM_INNER)), _resident_spec((MIX_WIDTH, D_MODEL))],
        out_specs=pl.BlockSpec((tm, D_MODEL), row),
        out_shape=jax.ShapeDtypeStruct((t, D_MODEL), F32),
        compiler_params=_params(("parallel",)), name="out_proj",
    )(x2d, attn, ya, yb, z, memo, w["ssmn"], w["wo"])


def _ffn_kernel(x_ref, prev_ref, next_ref, n2_ref, wg_ref, wu_ref, wd_ref, cw_ref, cb_ref, o_ref,
                g_ref, u_ref, act_ref, acc_ref):
    i = pl.program_id(1)
    ni = pl.num_programs(1)
    tm = x_ref.shape[1]
    n2 = n2_ref[...]

    def norm(v):
        ms = jnp.mean(v * v, axis=-1, keepdims=True)
        return (v * lax.rsqrt(ms + EPS) * n2).astype(BF16)

    hl = HALO_BF16
    pad = FFN_CONV // 2
    ts = FFN_SUB
    n_sub = tm // ts
    n_chunks = D_FF // FF_CHUNK

    hs, keeps = [], []
    for sub in range(n_sub):
        lo = norm(prev_ref[0]) if sub == 0 else norm(x_ref[0, sub * ts - hl:sub * ts, :])
        hi = norm(next_ref[0]) if sub == n_sub - 1 else norm(x_ref[0, (sub + 1) * ts:(sub + 1) * ts + hl, :])
        hs.append(jnp.concatenate([lo, norm(x_ref[0, sub * ts:(sub + 1) * ts, :]), hi], axis=0))
        keeps.append(((i > 0).astype(F32) if sub == 0 else None,
                      (i < ni - 1).astype(F32) if sub == n_sub - 1 else None))

    items = [(c, sub) for c in range(n_chunks) for sub in range(n_sub)]

    def up(it):
        c, sub = items[it]
        c0 = c * FF_CHUNK
        slot = it % 2
        keep_prev, keep_next = keeps[sub]
        for dst, wref in ((g_ref, wg_ref), (u_ref, wu_ref)):
            r = _dot(hs[sub], wref[:, c0:c0 + FF_CHUNK])
            dst[slot, 0:hl, :] = r[0:hl, :] if keep_prev is None else r[0:hl, :] * keep_prev
            dst[slot, hl:hl + ts, :] = r[hl:hl + ts, :]
            dst[slot, hl + ts:, :] = r[hl + ts:, :] if keep_next is None else r[hl + ts:, :] * keep_next

    def conv_act(it):
        c, _ = items[it]
        c0 = c * FF_CHUNK
        slot = it % 2
        outs = []
        for src, off in ((g_ref, 0), (u_ref, D_FF)):
            cw = cw_ref[:, off + c0:off + c0 + FF_CHUNK]
            cb = cb_ref[:, off + c0:off + c0 + FF_CHUNK]
            y = cb + src[slot, pl.ds(hl - pad, ts), :] * cw[0:1, :]
            for t in range(1, FFN_CONV):
                y = y + src[slot, pl.ds(hl - pad + t, ts), :] * cw[t:t + 1, :]
            outs.append(y)
        act_ref[it % ACT_SLOTS] = (_silu(outs[0]) * outs[1]).astype(BF16)

    def down(it):
        c, sub = items[it]
        c0 = c * FF_CHUNK
        part = _dot(act_ref[it % ACT_SLOTS], wd_ref[c0:c0 + FF_CHUNK, :])
        if c == 0:
            acc_ref[sub] = part
        else:
            acc_ref[sub] += part

    lag = ACT_SLOTS - 1
    n_items = len(items)
    done = 0
    up(0)
    for it in range(n_items):
        if it + 1 < n_items:
            up(it + 1)
        target = it - lag + 1 if it + 2 < n_items else it
        while done < target:
            down(done)
            done += 1
        conv_act(it)
    while done < n_items:
        down(done)
        done += 1
    for sub in range(n_sub):
        o_ref[0, sub * ts:(sub + 1) * ts, :] = x_ref[0, sub * ts:(sub + 1) * ts, :] + acc_ref[sub]


def _ffn(x1, w):
    b, s, _ = x1.shape
    tm = TM_FFN
    hl = HALO_BF16
    r = tm // hl
    nh = s // hl
    return pl.pallas_call(
        _ffn_kernel, grid=(b, s // tm),
        in_specs=[pl.BlockSpec((1, tm, D_MODEL), lambda bi, i: (bi, i, 0)),
                  pl.BlockSpec((1, hl, D_MODEL), lambda bi, i: (bi, jnp.maximum(i * r - 1, 0), 0)),
                  pl.BlockSpec((1, hl, D_MODEL), lambda bi, i: (bi, jnp.minimum((i + 1) * r, nh - 1), 0)),
                  _const_spec((1, D_MODEL)),
                  _resident_spec((D_MODEL, D_FF)), _resident_spec((D_MODEL, D_FF)), _resident_spec((D_FF, D_MODEL)),
                  _const_spec((FFN_CONV, 2 * D_FF)), _const_spec((1, 2 * D_FF))],
        out_specs=pl.BlockSpec((1, tm, D_MODEL), lambda bi, i: (bi, i, 0)),
        out_shape=jax.ShapeDtypeStruct((b, s, D_MODEL), F32),
        scratch_shapes=[pltpu.VMEM((2, FFN_SUB + 2 * hl, FF_CHUNK), F32),
                        pltpu.VMEM((2, FFN_SUB + 2 * hl, FF_CHUNK), F32),
                        pltpu.VMEM((ACT_SLOTS, FFN_SUB, FF_CHUNK), BF16),
                        pltpu.VMEM((tm // FFN_SUB, FFN_SUB, D_MODEL), F32)],
        compiler_params=_params(("parallel", "parallel")), name="ffn",
    )(x1, x1, x1, w["n2"], w["wg"], w["wu"], w["wd"], w["fcw"], w["fcb"])


def _prep_weights(max_seq, norm1_w, w_in, q_norm_w, k_norm_w, attn_sink, ssm_conv_w, ssm_conv_b, ssm_dt_bias,
                  ssm_A_log, ssm_D, ssm_norm_w, mem_norm_w, w_mem_kv, mq_norm_w, mk_norm_w, w_out, norm2_w,
                  w_ffn_up, ffn_conv_w, ffn_conv_b, w_ffn_down):
    o = 0
    wq = w_in[:, o:o + ATTN_WIDTH]; o += ATTN_WIDTH
    wk = w_in[:, o:o + KV_WIDTH]; o += KV_WIDTH
    wv = w_in[:, o:o + KV_WIDTH]; o += KV_WIDTH
    wz = w_in[:, o:o + SSM_INNER]; o += SSM_INNER
    wx = w_in[:, o:o + XBC_WIDTH]; o += XBC_WIDTH
    wdt = w_in[:, o:o + DT_WIDTH]; o += DT_WIDTH
    wmq = w_in[:, o:o + MEM_WIDTH]

    wq_p = wq.reshape(D_MODEL, N_KV_HEADS, GQA_GROUP, 2, HALF).transpose(0, 2, 3, 1, 4).reshape(D_MODEL, ATTN_WIDTH)
    wk_p = wk.reshape(D_MODEL, N_KV_HEADS, 2, HALF).transpose(0, 2, 1, 3).reshape(D_MODEL, KV_WIDTH)
    qg = jnp.broadcast_to(q_norm_w.reshape(1, 2, 1, HALF), (GQA_GROUP, 2, N_KV_HEADS, HALF)).reshape(ATTN_WIDTH)
    kg = jnp.broadcast_to(k_norm_w.reshape(2, 1, HALF), (2, N_KV_HEADS, HALF)).reshape(KV_WIDTH)
    qkg = jnp.concatenate([qg * (HEAD_DIM ** -0.5 * LOG2E), kg]).reshape(1, QK_WIDTH)

    cq = np.arange(ATTN_WIDTH)
    eq = (cq // 256) * N_KV_HEADS + (cq % LANES) // HALF
    ck = np.arange(KV_WIDTH)
    ek = N_Q_HEADS + (ck % LANES) // HALF
    e_all = np.concatenate([eq, ek])
    hsum = np.zeros((QK_WIDTH, LANES), np.float32)
    hsum[np.arange(QK_WIDTH), e_all] = 1.0
    hexp = np.concatenate([hsum.T, hsum.T], axis=0)

    inv = ROPE_THETA ** (-jnp.arange(0, HEAD_DIM, 2, dtype=F32) / HEAD_DIM)
    ang = jnp.arange(max_seq, dtype=F32)[:, None] * inv[None, :]
    cos = jnp.tile(jnp.cos(ang), (1, LANES // HALF))
    sin = jnp.tile(jnp.sin(ang), (1, LANES // HALF))

    wo_a = w_out[:ATTN_WIDTH].reshape(N_KV_HEADS, GQA_GROUP, HEAD_DIM, D_MODEL).transpose(1, 0, 2, 3)
    wo_a = wo_a.reshape(ATTN_WIDTH, D_MODEL)

    r48 = np.arange(3 * DT_WIDTH) % DT_WIDTH
    onehot = np.zeros((SSM_HEADS, 3 * DT_WIDTH, 2 * SSM_CHUNK), np.float32)
    for hd in range(SSM_HEADS):
        onehot[hd, r48 == hd, :SSM_CHUNK] = 1.0
        onehot[hd, r48 == SSM_HEADS + hd, SSM_CHUNK:] = 1.0
    r32 = np.arange(2 * DT_WIDTH) % DT_WIDTH
    col_head = np.arange(SSM_INNER) // SSM_HEAD_DIM
    sel_f = (r32[:, None] == col_head[None, :]).astype(np.float32)
    sel_b = (r32[:, None] == SSM_HEADS + col_head[None, :]).astype(np.float32)
    return {
        "onehot": jnp.asarray(onehot, BF16), "sel_f": jnp.asarray(sel_f, BF16), "sel_b": jnp.asarray(sel_b, BF16),
        "n1": norm1_w.reshape(1, D_MODEL),
        "wqk": jnp.concatenate([wq_p, wk_p], axis=1).astype(BF16),
        "wv": wv.astype(BF16), "wz": wz.astype(BF16), "wx": wx.astype(BF16),
        "wdtt": wdt.T.astype(BF16), "wmq": wmq.astype(BF16),
        "hsum": jnp.asarray(hsum, BF16), "hexp": jnp.asarray(hexp, BF16),
        "qkg": qkg, "mqg": mq_norm_w.reshape(1, LANES), "cos": cos, "sin": sin,
        "sink": attn_sink,
        "memn": mem_norm_w.reshape(1, D_MODEL), "wmemkv": w_mem_kv.astype(BF16), "mkg": mk_norm_w.reshape(1, LANES),
        "convw": ssm_conv_w, "convb": ssm_conv_b.reshape(1, XBC_WIDTH),
        "dtb_c": ssm_dt_bias.reshape(DT_WIDTH, 1), "alog_c": ssm_A_log.reshape(DT_WIDTH, 1),
        "dskip": jnp.repeat(ssm_D, SSM_HEAD_DIM).reshape(1, SSM_INNER),
        "ssmn": ssm_norm_w.reshape(1, SSM_INNER),
        "wo": jnp.concatenate([wo_a, w_out[ATTN_WIDTH + SSM_INNER:],
                               w_out[ATTN_WIDTH:ATTN_WIDTH + SSM_INNER]], axis=0).astype(BF16),
        "n2": norm2_w.reshape(1, D_MODEL),
        "wg": w_ffn_up[:, :D_FF].astype(BF16), "wu": w_ffn_up[:, D_FF:].astype(BF16),
        "wd": w_ffn_down.astype(BF16),
        "fcw": ffn_conv_w, "fcb": ffn_conv_b.reshape(1, 2 * D_FF),
    }


def _encoder_layer(x, mem, w):
    b, s, _ = x.shape
    t = b * s
    x2d = x.reshape(t, D_MODEL)
    q, k, v, z, xbc, dtt, mq = _in_proj(x2d, s, w)
    mk, mv = _mem_kv(mem, w)
    attn, memo = _attention(q.reshape(b, s, -1), k.reshape(b, s, -1), v.reshape(b, s, -1),
                            mq.reshape(b, s, -1), mk, mv, w["sink"])
    ya, yb = _ssd_scan(xbc.reshape(b, s, -1), dtt, w)
    x1 = _out_proj(x2d, attn.reshape(t, -1), ya.reshape(t, -1), yb.reshape(t, -1), z, memo.reshape(t, -1), w)
    return _ffn(x1.reshape(b, s, D_MODEL), w)


def kernel(x_prompt, x_sample, mem_prompt, mem_sample, norm1_w, w_in, q_norm_w, k_norm_w, attn_sink, ssm_conv_w, ssm_conv_b, ssm_dt_bias, ssm_A_log, ssm_D, ssm_norm_w, mem_norm_w, w_mem_kv, mq_norm_w, mk_norm_w, w_out, norm2_w, w_ffn_up, ffn_conv_w, ffn_conv_b, w_ffn_down):
    weights = (norm1_w, w_in, q_norm_w, k_norm_w, attn_sink, ssm_conv_w, ssm_conv_b, ssm_dt_bias,
               ssm_A_log, ssm_D, ssm_norm_w, mem_norm_w, w_mem_kv, mq_norm_w, mk_norm_w, w_out,
               norm2_w, w_ffn_up, ffn_conv_w, ffn_conv_b, w_ffn_down)
    depth = norm1_w.shape[0]
    max_seq = max(x_prompt.shape[1], x_sample.shape[1])
    y_prompt, y_sample = x_prompt, x_sample
    for layer in range(depth):
        w = _prep_weights(max_seq, *[p[layer] for p in weights])
        y_prompt = _encoder_layer(y_prompt, mem_prompt, w)
        y_sample = _encoder_layer(y_sample, mem_sample, w)
    return (y_prompt, y_sample)
```
